```python
import math
import jax, jax.numpy as jnp
from jax import lax
import numpy as np

D_MODEL = 1024
BATCH = 8
SEQ = 2048
DEPTH = 4
DEC_BATCH = 1
DEC_SEQ = 16384
PAST_LEN = 128

N_META = 16
RMS_EPS = 1e-6
ROPE_THETA = 10000.0
D_FF = 2816

ATT_HEADS = 4
ATT_HEAD_DIM = 64
ATT_V_DIM = 2 * ATT_HEAD_DIM
ATT_WIDTH = ATT_HEADS * ATT_V_DIM
Q_BLOCK = 128

POOL_WINDOWS = (2, 4, 8, 16)
POOL_GROUPS = len(POOL_WINDOWS)
POOL_GROUP_DIM = 64
POOL_WIDTH = POOL_GROUPS * POOL_GROUP_DIM

GLA_HEADS = 4
GLA_HEAD_DIM = 64
GLA_WIDTH = GLA_HEADS * GLA_HEAD_DIM
GLA_GATE_RANK = 16
GLA_GATE_TEMP = 16.0
GLA_CHUNK = 64

MIX_WIDTH = ATT_WIDTH + POOL_WIDTH + GLA_WIDTH

IN_SPLITS = (
    ATT_HEADS * 2 * ATT_HEAD_DIM,
    ATT_HEADS * 2 * ATT_HEAD_DIM,
    ATT_WIDTH,
    POOL_WIDTH,
    GLA_WIDTH,
    GLA_WIDTH,
    GLA_WIDTH,
    GLA_WIDTH,
    2 * GLA_GATE_RANK,
)
IN_COLS = sum(IN_SPLITS)
IN_OFFSETS = tuple(int(o) for o in np.cumsum(IN_SPLITS)[:-1])

kernel_name = "hybrid_bidir_encoder_diffattn_pool_gla"


def rmsnorm(x, g):
    xf = x.astype(jnp.float32)
    y = xf * lax.rsqrt(jnp.mean(xf * xf, axis=-1, keepdims=True) + RMS_EPS)
    return (y * g.astype(jnp.float32)).astype(x.dtype)


def rope_tables(L):
    pos = jnp.arange(L, dtype=jnp.float32)
    inv_freq = ROPE_THETA ** (-jnp.arange(0, ATT_HEAD_DIM, 2, dtype=jnp.float32) / ATT_HEAD_DIM)
    ang = pos[:, None] * inv_freq[None, :]
    return jnp.cos(ang), jnp.sin(ang)


def apply_rope(x, cos, sin):
    xf = x.astype(jnp.float32)
    x1, x2 = jnp.split(xf, 2, axis=-1)
    c = cos[None, :, None, None, :]
    s = sin[None, :, None, None, :]
    return jnp.concatenate([x1 * c - x2 * s, x2 * c + x1 * s], axis=-1).astype(x.dtype)


def swiglu(x, w_in, w_out):
    g, u = jnp.split(x @ w_in, 2, axis=-1)
    return (jax.nn.silu(g) * u) @ w_out


def diff_attention(q, k, v, lam, sub_norm, lam_init):
    B, L = q.shape[:2]
    scale = ATT_HEAD_DIM ** -0.5

    def attend(qb):
        s = jnp.einsum('bqhcd,bkhcd->bhcqk', qb, k).astype(jnp.float32) * scale
        p = jax.nn.softmax(s, axis=-1)
        w = p[:, :, 0] - lam * p[:, :, 1]
        return jnp.einsum('bhqk,bkhe->bqhe', w.astype(v.dtype), v)

    o_meta = attend(q[:, :N_META])
    qr = q[:, N_META:]
    Lr = qr.shape[1]
    nb = Lr // Q_BLOCK
    qr = qr.reshape(B, nb, Q_BLOCK, ATT_HEADS, 2, ATT_HEAD_DIM).transpose(1, 0, 2, 3, 4, 5)
    o_r = lax.map(attend, qr)
    o_r = o_r.transpose(1, 0, 2, 3, 4).reshape(B, Lr, ATT_HEADS, ATT_V_DIM)
    o = jnp.concatenate([o_meta, o_r], axis=1)
    o = rmsnorm(o, sub_norm) * (1.0 - lam_init)
    return o.reshape(B, L, ATT_WIDTH)


def multiscale_pool(u, w, scale):
    B, L, _ = u.shape
    uf = u.astype(jnp.float32)
    cs = jnp.concatenate([jnp.zeros((B, 1, POOL_WIDTH), jnp.float32), jnp.cumsum(uf, axis=1)], axis=1)
    t = np.arange(L)
    outs = []
    for g, win in enumerate(POOL_WINDOWS):
        lo = win // 2
        hi = win - lo - 1
        start = np.clip(t - lo, 0, L)
        end = np.clip(t + hi + 1, 0, L)
        cnt = jnp.asarray((end - start).astype(np.float32))
        sl = slice(g * POOL_GROUP_DIM, (g + 1) * POOL_GROUP_DIM)
        csg = cs[..., sl]
        mean = (jnp.take(csg, jnp.asarray(end), axis=1) - jnp.take(csg, jnp.asarray(start), axis=1)) / cnt[None, :, None]
        outs.append(mean - uf[..., sl])
    p = jnp.stack(outs, axis=2).astype(u.dtype)
    y = jnp.einsum('blgc,gcd->blgd', p, w).reshape(B, L, POOL_WIDTH)
    return y * scale


def gla_chunk_scan(q, k, v, g):
    C = q.shape[3]
    mask = jnp.tril(jnp.ones((C, C), dtype=bool))[:, :, None]

    def step(S, inp):
        qc, kc, vc, gc = inp
        b = jnp.cumsum(gc, axis=2)
        b_last = b[:, :, -1:, :]
        o_inter = jnp.einsum('bhtd,bhde->bhte', qc * jnp.exp(b), S)
        diff = b[:, :, :, None, :] - b[:, :, None, :, :]
        decay = jnp.where(mask, jnp.exp(jnp.where(mask, diff, 0.0)), 0.0)
        A = jnp.einsum('bhtd,bhsd,bhtsd->bhts', qc, kc, decay)
        o_intra = jnp.einsum('bhts,bhse->bhte', A, vc)
        S_new = jnp.exp(b_last)[:, :, 0, :, None] * S + jnp.einsum('bhsd,bhse->bhde', kc * jnp.exp(b_last - b), vc)
        return S_new, o_inter + o_intra

    N, B, H, _, dk = q.shape
    S0 = jnp.zeros((B, H, dk, v.shape[-1]), jnp.float32)
    _, o = lax.scan(step, S0, (q, k, v, g))
    return o


def _chunk(x):
    B, Lp, H, d = x.shape
    return x.reshape(B, Lp // GLA_CHUNK, GLA_CHUNK, H, d).transpose(1, 0, 3, 2, 4)


def _unchunk(x):
    N, B, H, C, d = x.shape
    return x.transpose(1, 0, 3, 2, 4).reshape(B, N * C, H, d)


def gla_mixer(q, k, v, out_gate, gate_lr, gate_up, gate_bias, out_norm):
    B, L, H, dk = q.shape
    logit = jnp.einsum('blzr,zrc->blzc', gate_lr, gate_up) + gate_bias
    log_a = (jax.nn.log_sigmoid(logit.astype(jnp.float32)) / GLA_GATE_TEMP).reshape(B, L, 2, H, dk)
    pad = GLA_CHUNK - N_META

    def prep(x):
        return jnp.pad(x.astype(jnp.float32), ((0, 0), (pad, 0), (0, 0), (0, 0)))

    def flip(x):
        return jnp.flip(x, axis=1)

    qp = prep(q) * (dk ** -0.5)
    kp = prep(k)
    vp = prep(v)
    gf = prep(log_a[:, :, 0])
    gb = prep(log_a[:, :, 1])
    o_f = _unchunk(gla_chunk_scan(_chunk(qp), _chunk(kp), _chunk(vp), _chunk(gf)))
    o_b = flip(_unchunk(gla_chunk_scan(_chunk(flip(qp)), _chunk(flip(kp)), _chunk(flip(vp)), _chunk(flip(gb)))))
    o = (o_f + o_b)[:, pad:]
    o = rmsnorm(o, out_norm) * jax.nn.silu(out_gate.astype(jnp.float32))
    return o.reshape(B, L, GLA_WIDTH).astype(q.dtype)


def token_mixing(h, w_in, w_out, q_norm, k_norm, lq1, lk1, lq2, lk2, sub_norm,
                 pool_w, pool_scale, gate_up, gate_bias, out_norm, cos, sin, lam_init):
    B, L, _ = h.shape
    z = h @ w_in
    aq, ak, av, pu, gq, gk, gv, gg, glr = jnp.split(z, IN_OFFSETS, axis=-1)
    aq = apply_rope(rmsnorm(aq.reshape(B, L, ATT_HEADS, 2, ATT_HEAD_DIM), q_norm), cos, sin)
    ak = apply_rope(rmsnorm(ak.reshape(B, L, ATT_HEADS, 2, ATT_HEAD_DIM), k_norm), cos, sin)
    av = av.reshape(B, L, ATT_HEADS, ATT_V_DIM)
    f32 = jnp.float32
    lam = (jnp.exp(jnp.sum(lq1.astype(f32) * lk1.astype(f32)))
           - jnp.exp(jnp.sum(lq2.astype(f32) * lk2.astype(f32))) + lam_init)
    o_att = diff_attention(aq, ak, av, lam, sub_norm, lam_init)
    o_pool = multiscale_pool(pu, pool_w, pool_scale)
    hd = (B, L, GLA_HEADS, GLA_HEAD_DIM)
    o_gla = gla_mixer(gq.reshape(hd), gk.reshape(hd), gv.reshape(hd), gg.reshape(hd),
                      glr.reshape(B, L, 2, GLA_GATE_RANK), gate_up, gate_bias, out_norm)
    return jnp.concatenate([o_att, o_pool.astype(o_att.dtype), o_gla.astype(o_att.dtype)], axis=-1) @ w_out


def trunk(x, meta_tokens, ffn1_norm, ffn1_w_in, ffn1_w_out, mix_norm, w_in, w_out,
          attn_q_norm, attn_k_norm, lambda_q1, lambda_k1, lambda_q2, lambda_k2, attn_sub_norm,
          pool_w, pool_scale, gla_gate_up, gla_gate_bias, gla_out_norm,
          ffn2_norm, ffn2_w_in, ffn2_w_out):
    B = x.shape[0]
    meta = jnp.broadcast_to(meta_tokens.astype(x.dtype)[None], (B, N_META, D_MODEL))
    h = jnp.concatenate([meta, x], axis=1)
    cos, sin = rope_tables(h.shape[1])
    for l in range(DEPTH):
        lam_init = 0.8 - 0.6 * math.exp(-0.3 * l)
        h = h + 0.5 * swiglu(rmsnorm(h, ffn1_norm[l]), ffn1_w_in[l], ffn1_w_out[l])
        h = h + token_mixing(rmsnorm(h, mix_norm[l]), w_in[l], w_out[l],
                             attn_q_norm[l], attn_k_norm[l], lambda_q1[l], lambda_k1[l],
                             lambda_q2[l], lambda_k2[l], attn_sub_norm[l],
                             pool_w[l], pool_scale[l], gla_gate_up[l], gla_gate_bias[l],
                             gla_out_norm[l], cos, sin, lam_init)
        h = h + 0.5 * swiglu(rmsnorm(h, ffn2_norm[l]), ffn2_w_in[l], ffn2_w_out[l])
    return h[:, N_META:]


def setup_inputs(seed: int = 0) -> dict:
    key = jax.random.key(seed)
    ks = jax.random.split(key, 32)
    f32 = jnp.float32

    def nrm(k, shape, scale):
        return jax.random.normal(k, shape, f32) * scale

    def gain(k, shape):
        return 1.0 + 0.02 * jax.random.normal(k, shape, f32)

    return {
        "x_prompt": nrm(ks[0], (BATCH, SEQ, D_MODEL), 1.0),
        "x_sample": nrm(ks[1], (DEC_BATCH, DEC_SEQ, D_MODEL), 1.0),
        "meta_tokens": nrm(ks[2], (N_META, D_MODEL), 1.0),
        "ffn1_norm": gain(ks[3], (DEPTH, D_MODEL)),
        "ffn1_w_in": nrm(ks[4], (DEPTH, D_MODEL, 2 * D_FF), D_MODEL ** -0.5),
        "ffn1_w_out": nrm(ks[5], (DEPTH, D_FF, D_MODEL), D_FF ** -0.5),
        "mix_norm": gain(ks[6], (DEPTH, D_MODEL)),
        "w_in": nrm(ks[7], (DEPTH, D_MODEL, IN_COLS), D_MODEL ** -0.5),
        "w_out": nrm(ks[8], (DEPTH, MIX_WIDTH, D_MODEL), MIX_WIDTH ** -0.5),
        "attn_q_norm": gain(ks[9], (DEPTH, ATT_HEAD_DIM)),
        "attn_k_norm": gain(ks[10], (DEPTH, ATT_HEAD_DIM)),
        "lambda_q1": nrm(ks[11], (DEPTH, ATT_HEAD_DIM), 0.1),
        "lambda_k1": nrm(ks[12], (DEPTH, ATT_HEAD_DIM), 0.1),
        "lambda_q2": nrm(ks[13], (DEPTH, ATT_HEAD_DIM), 0.1),
        "lambda_k2": nrm(ks[14], (DEPTH, ATT_HEAD_DIM), 0.1),
        "attn_sub_norm": gain(ks[15], (DEPTH, ATT_V_DIM)),
        "pool_w": nrm(ks[16], (DEPTH, POOL_GROUPS, POOL_GROUP_DIM, POOL_GROUP_DIM), POOL_GROUP_DIM ** -0.5),
        "pool_scale": gain(ks[17], (DEPTH, POOL_WIDTH)),
        "gla_gate_up": nrm(ks[18], (DEPTH, 2, GLA_GATE_RANK, GLA_WIDTH), GLA_GATE_RANK ** -0.5),
        "gla_gate_bias": nrm(ks[19], (DEPTH, 2, GLA_WIDTH), 0.1),
        "gla_out_norm": gain(ks[20], (DEPTH, GLA_HEAD_DIM)),
        "ffn2_norm": gain(ks[21], (DEPTH, D_MODEL)),
        "ffn2_w_in": nrm(ks[22], (DEPTH, D_MODEL, 2 * D_FF), D_MODEL ** -0.5),
        "ffn2_w_out": nrm(ks[23], (DEPTH, D_FF, D_MODEL), D_FF ** -0.5),
    }


def reference(x_prompt, x_sample, meta_tokens, ffn1_norm, ffn1_w_in, ffn1_w_out, mix_norm, w_in, w_out,
              attn_q_norm, attn_k_norm, lambda_q1, lambda_k1, lambda_q2, lambda_k2, attn_sub_norm,
              pool_w, pool_scale, gla_gate_up, gla_gate_bias, gla_out_norm,
              ffn2_norm, ffn2_w_in, ffn2_w_out):
    y_prompt = trunk(x_prompt, meta_tokens, ffn1_norm, ffn1_w_in, ffn1_w_out, mix_norm, w_in, w_out,
                     attn_q_norm, attn_k_norm, lambda_q1, lambda_k1, lambda_q2, lambda_k2, attn_sub_norm,
                     pool_w, pool_scale, gla_gate_up, gla_gate_bias, gla_out_norm,
                     ffn2_norm, ffn2_w_in, ffn2_w_out)
    y_sample = trunk(x_sample, meta_tokens, ffn1_norm, ffn1_w_in, ffn1_w_out, mix_norm, w_in, w_out,
                     attn_q_norm, attn_k_norm, lambda_q1, lambda_k1, lambda_q2, lambda_k2, attn_sub_norm,
                     pool_w, pool_scale, gla_gate_up, gla_gate_bias, gla_out_norm,
                     ffn2_norm, ffn2_w_in, ffn2_w_out)
    return (y_prompt, y_sample)
```

```python
import functools
import math

import jax
import jax.numpy as jnp
from jax import lax
from jax.experimental import pallas as pl
from jax.experimental.pallas import tpu as pltpu

F32 = jnp.float32
BF16 = jnp.bfloat16

D_MODEL = 1024
D_FF = 2816
DEPTH = 4
N_META = 16
RMS_EPS = 1e-6
ROPE_THETA = 10000.0

ATT_HEADS = 4
ATT_HEAD_DIM = 64
ATT_V_DIM = 2 * ATT_HEAD_DIM
ATT_WIDTH = ATT_HEADS * ATT_V_DIM
ATT_V_EXT = 2 * ATT_V_DIM

POOL_WINDOWS = (2, 4, 8, 16)
POOL_GROUP_DIM = 64
POOL_WIDTH = len(POOL_WINDOWS) * POOL_GROUP_DIM
POOL_HALO = 8

GLA_HEADS = 4
GLA_HEAD_DIM = 64
GLA_WIDTH = GLA_HEADS * GLA_HEAD_DIM
GLA_GATE_RANK = 16
GLA_GATE_TEMP = 16.0
GLA_CHUNK = 64
GLA_SUB = 16
GLA_NSUB = GLA_CHUNK // GLA_SUB
GLA_EXP_CLAMP = 80.0

MIX_WIDTH = ATT_WIDTH + POOL_WIDTH + GLA_WIDTH
IN_COLS = 4 * ATT_WIDTH - ATT_WIDTH + POOL_WIDTH + 4 * GLA_WIDTH + 2 * GLA_GATE_RANK
GATE_COL0 = IN_COLS - 2 * GLA_GATE_RANK
IN_COLS_PAD = GATE_COL0 + 128

V7X_LANES = 128
V7X_MXU_DIM = 256
SEQ_ALIGN = 128
SHORT_SEQ_ROWS = 2304
ATT_TILE_CAP = 1280
ROW_TILE_CAP = 704
GLA_TILE_CAP = 1280
FFN_TILE_CAP = 1280
VMEM_LIMIT_BYTES = 56 * 1024 * 1024
FF_TILE = 256
LOG2E = 1.4426950408889634


def _divisor_tile(n, cap, mult):
    best = None
    for t in range(mult, min(n, cap) + 1, mult):
        if n % t == 0:
            best = t
    assert best is not None, (n, cap, mult)
    return best


def _plan(seq_len):
    if seq_len <= SHORT_SEQ_ROWS:
        lp = -(-seq_len // SEQ_ALIGN) * SEQ_ALIGN
        tk = lp
    else:
        tk = ATT_TILE_CAP
        lp = -(-seq_len // tk) * tk
    tq = _divisor_tile(lp, max(16, min(ATT_TILE_CAP, (ATT_TILE_CAP * ATT_TILE_CAP) // tk)), 16)
    return dict(
        lp=lp,
        row_tile=_divisor_tile(lp, ROW_TILE_CAP, 16),
        tq=tq,
        tk=tk,
        gla_tile=_divisor_tile(lp, GLA_TILE_CAP, GLA_CHUNK),
    )


def _params(sem):
    return pltpu.CompilerParams(dimension_semantics=sem, vmem_limit_bytes=VMEM_LIMIT_BYTES)


def _dot(a, b):
    return jnp.dot(a, b, preferred_element_type=F32)


def _dot_nt(a, b):
    return lax.dot_general(a, b, (((1,), (1,)), ((), ())), preferred_element_type=F32)


def _dot_tn(a, b):
    return lax.dot_general(a, b, (((0,), (0,)), ((), ())), preferred_element_type=F32)


def _split2(x):
    hi = x.astype(BF16)
    lo = (x - hi.astype(F32)).astype(BF16)
    return hi, lo


def _group_mean_sq(x, bd):
    hi, lo = _split2(x * x)
    outs = []
    for c in range(x.shape[1] // V7X_MXU_DIM):
        sl = slice(V7X_MXU_DIM * c, V7X_MXU_DIM * (c + 1))
        outs.append(_dot(hi[:, sl], bd) + _dot(lo[:, sl], bd))
    return outs[0] if len(outs) == 1 else jnp.concatenate(outs, axis=1)


def _ffn_kernel(nf, x_ref, g_ref, wg_ref, wu_ref, wo_ref, o_ref, xn_ref, acc_ref):
    j = pl.program_id(1)

    @pl.when(j == 0)
    def _():
        x = x_ref[...]
        ms = jnp.mean(x * x, axis=-1, keepdims=True)
        xn_ref[...] = (x * lax.rsqrt(ms + RMS_EPS) * g_ref[...]).astype(BF16)
        acc_ref[...] = jnp.zeros_like(acc_ref)

    xn = xn_ref[...]
    g = _dot(xn, wg_ref[...])
    u = _dot(xn, wu_ref[...])
    a = (g * jax.nn.sigmoid(g) * u).astype(BF16)
    acc_ref[...] += _dot(a, wo_ref[...])

    @pl.when(j == nf - 1)
    def _():
        o_ref[...] = x_ref[...] + 0.5 * acc_ref[...]


def _ffn(h2d, layer, norm, w_in, w_out):
    rows = h2d.shape[0]
    tm = _divisor_tile(rows, FFN_TILE_CAP, 16)
    nf = D_FF // FF_TILE
    return pl.pallas_call(
        functools.partial(_ffn_kernel, nf),
        grid=(rows // tm, nf),
        in_specs=[
            pl.BlockSpec((tm, D_MODEL), lambda i, j: (i, 0)),
            pl.BlockSpec((None, 1, D_MODEL), lambda i, j: (layer, 0, 0)),
            pl.BlockSpec((None, D_MODEL, FF_TILE), lambda i, j: (layer, 0, j)),
            pl.BlockSpec((None, D_MODEL, FF_TILE), lambda i, j: (layer, 0, j + nf)),
            pl.BlockSpec((None, FF_TILE, D_MODEL), lambda i, j: (layer, j, 0)),
        ],
        out_specs=pl.BlockSpec((tm, D_MODEL), lambda i, j: (i, 0)),
        out_shape=jax.ShapeDtypeStruct((rows, D_MODEL), F32),
        scratch_shapes=[pltpu.VMEM((tm, D_MODEL), BF16), pltpu.VMEM((tm, D_MODEL), F32)],
        compiler_params=_params(("parallel", "arbitrary")),
        name="ffn",
    )(h2d, norm, w_in, w_in, w_out)


def _rope128(x, cos, sin_signed, first_half):
    partner = jnp.where(first_half, pltpu.roll(x, 96, 1), pltpu.roll(x, 32, 1))
    return x * cos + partner * sin_signed


def _proj_kernel(seq_len, tm, h_ref, g_ref, w_ref, cos_ref, sin_ref, qg_ref, kg_ref, bd_ref, gu_ref, gbias_ref,
                 q_ref, k_ref, v_ref, pu_ref, gq_ref, gk_ref, gv_ref, gg_ref, gf_ref, gb_ref):
    i = pl.program_id(1)
    x = h_ref[...]
    pos = i * tm + lax.broadcasted_iota(jnp.int32, (tm, 1), 0)
    valid = pos < seq_len
    ms = jnp.mean(x * x, axis=-1, keepdims=True)
    xn = jnp.where(valid, x * lax.rsqrt(ms + RMS_EPS) * g_ref[...], 0.0).astype(BF16)

    bd = bd_ref[...]
    cos = cos_ref[...]
    sin = sin_ref[...]
    lane = lax.broadcasted_iota(jnp.int32, (tm, V7X_LANES), 1)
    first_half = (lane % ATT_HEAD_DIM) < (ATT_HEAD_DIM // 2)

    def qk_path(col0, gain_ref, out_ref, out_scale):
        z = _dot(xn, w_ref[:, col0:col0 + ATT_WIDTH])
        zn = z * lax.rsqrt(_group_mean_sq(z, bd) + RMS_EPS)
        gain = gain_ref[...]
        for c in range(ATT_WIDTH // V7X_LANES):
            sl = slice(V7X_LANES * c, V7X_LANES * (c + 1))
            r = _rope128(zn[:, sl] * gain, cos, sin, first_half)
            out_ref[:, sl] = (r * out_scale).astype(BF16)

    qk_path(0, qg_ref, q_ref, ATT_HEAD_DIM ** -0.5 * LOG2E)
    qk_path(ATT_WIDTH, kg_ref, k_ref, 1.0)

    zv = _dot(xn, w_ref[:, 2 * ATT_WIDTH:3 * ATT_WIDTH])
    ones_col = jnp.where(jnp.logical_and(lane == 0, valid), 1.0, 0.0).astype(BF16)
    for hh in range(ATT_HEADS):
        v_ref[:, ATT_V_EXT * hh:ATT_V_EXT * hh + ATT_V_DIM] = zv[:, ATT_V_DIM * hh:ATT_V_DIM * (hh + 1)].astype(BF16)
        v_ref[:, ATT_V_EXT * hh + ATT_V_DIM:ATT_V_EXT * (hh + 1)] = ones_col

    c0 = 3 * ATT_WIDTH
    pu_ref[...] = _dot(xn, w_ref[:, c0:c0 + POOL_WIDTH])
    c0 += POOL_WIDTH
    gq_ref[...] = (_dot(xn, w_ref[:, c0:c0 + GLA_WIDTH]) * (GLA_HEAD_DIM ** -0.5)).astype(BF16)
    c0 += GLA_WIDTH
    gk_ref[...] = _dot(xn, w_ref[:, c0:c0 + GLA_WIDTH]).astype(BF16)
    c0 += GLA_WIDTH
    gv_ref[...] = _dot(xn, w_ref[:, c0:c0 + GLA_WIDTH]).astype(BF16)
    c0 += GLA_WIDTH
    gg_ref[...] = _dot(xn, w_ref[:, c0:c0 + GLA_WIDTH])

    glr = _dot(xn, w_ref[:, GATE_COL0:IN_COLS_PAD])
    glr_hi, glr_lo = _split2(glr)
    gu_hi, gu_lo = _split2(gu_ref[...])
    logit = _dot(glr_hi, gu_hi) + _dot(glr_lo, gu_hi) + _dot(glr_hi, gu_lo) + gbias_ref[...]
    log_sig = jnp.minimum(logit, 0.0) - jnp.log(1.0 + jnp.exp(-jnp.abs(logit)))
    log_a = log_sig * (1.0 / GLA_GATE_TEMP)
    gf_ref[...] = log_a[:, :GLA_WIDTH]
    gb_ref[...] = log_a[:, GLA_WIDTH:]


def _proj(h, layer, seq_len, plan, prep):
    b, lp, _ = h.shape
    tm = plan["row_tile"]
    row = lambda w: pl.BlockSpec((None, tm, w), lambda bb, i: (bb, i, 0))
    par = lambda *shape: pl.BlockSpec((None,) + shape, lambda bb, i: (layer,) + (0,) * len(shape))
    out_widths = [(ATT_WIDTH, BF16), (ATT_WIDTH, BF16), (ATT_HEADS * ATT_V_EXT, BF16), (POOL_WIDTH, F32),
                  (GLA_WIDTH, BF16), (GLA_WIDTH, BF16), (GLA_WIDTH, BF16), (GLA_WIDTH, F32),
                  (GLA_WIDTH, F32), (GLA_WIDTH, F32)]
    return pl.pallas_call(
        functools.partial(_proj_kernel, seq_len, tm),
        grid=(b, lp // tm),
        in_specs=[
            row(D_MODEL),
            par(1, D_MODEL),
            par(D_MODEL, IN_COLS_PAD),
            pl.BlockSpec((tm, V7X_LANES), lambda bb, i: (i, 0)),
            pl.BlockSpec((tm, V7X_LANES), lambda bb, i: (i, 0)),
            par(1, V7X_LANES),
            par(1, V7X_LANES),
            pl.BlockSpec((V7X_MXU_DIM, V7X_MXU_DIM), lambda bb, i: (0, 0)),
            par(V7X_LANES, 2 * GLA_WIDTH),
            par(1, 2 * GLA_WIDTH),
        ],
        out_specs=[row(w) for w, _ in out_widths],
        out_shape=[jax.ShapeDtypeStruct((b, lp, w), dt) for w, dt in out_widths],
        compiler_params=_params(("parallel", "parallel")),
        name="proj",
    )(h, prep["mix_norm"], prep["w_in"], prep["cos"][lp], prep["sin"][lp], prep["q_gain"], prep["k_gain"],
      prep["bd"], prep["gate_up"], prep["gate_bias"])


def _attn_kernel(seq_len, tk, nk, lam_init, q_ref, k_ref, v_ref, lam_ref, sn_ref, o_ref,
                 m0_ref, m1_ref, acc0_ref, acc1_ref):
    ik = pl.program_id(3)

    @pl.when(ik == 0)
    def _():
        m0_ref[...] = jnp.full_like(m0_ref, -jnp.inf)
        m1_ref[...] = jnp.full_like(m1_ref, -jnp.inf)
        acc0_ref[...] = jnp.zeros_like(acc0_ref)
        acc1_ref[...] = jnp.zeros_like(acc1_ref)

    def step(mask_tail):
        q = q_ref[...]
        k = k_ref[...]
        v = v_ref[...]
        lane = lax.broadcasted_iota(jnp.int32, k.shape, 1)
        halves = (jnp.where(lane < ATT_HEAD_DIM, k, jnp.zeros_like(k)),
                  jnp.where(lane >= ATT_HEAD_DIM, k, jnp.zeros_like(k)))
        for kc, m_ref, acc_ref in ((halves[0], m0_ref, acc0_ref), (halves[1], m1_ref, acc1_ref)):
            s = _dot_nt(q, kc)
            if mask_tail:
                col = ik * tk + lax.broadcasted_iota(jnp.int32, s.shape, 1)
                s = jnp.where(col < seq_len, s, -jnp.inf)
            m_prev = m_ref[...]
            m_new = jnp.maximum(m_prev, jnp.max(s, axis=-1, keepdims=True))
            alpha = jnp.exp2(m_prev - m_new)
            p = jnp.exp2(s - m_new[:, :1])
            acc_ref[...] = alpha[:, :1] * acc_ref[...] + _dot(p.astype(BF16), v)
            m_ref[...] = m_new

    if (nk - 1) * tk + tk > seq_len:
        pl.when(ik < nk - 1)(lambda: step(False))
        pl.when(ik == nk - 1)(lambda: step(True))
    else:
        step(False)

    @pl.when(ik == nk - 1)
    def _():
        lp = lam_ref[...]
        a1 = jnp.sum(lp[0:1] * lp[1:2], axis=-1, keepdims=True)
        a2 = jnp.sum(lp[2:3] * lp[3:4], axis=-1, keepdims=True)
        lam = jnp.exp(a1) - jnp.exp(a2) + lam_init
        acc0 = acc0_ref[...]
        acc1 = acc1_ref[...]
        o = (acc0[:, :ATT_V_DIM] / acc0[:, ATT_V_DIM:ATT_V_DIM + 1]
             - lam * (acc1[:, :ATT_V_DIM] / acc1[:, ATT_V_DIM:ATT_V_DIM + 1]))
        ms = jnp.mean(o * o, axis=-1, keepdims=True)
        o = o * lax.rsqrt(ms + RMS_EPS) * sn_ref[...] * (1.0 - lam_init)
        o_ref[...] = o.astype(BF16)


def _attn(q, k, v, layer, seq_len, plan, prep, lam_init):
    b, lp, _ = q.shape
    tq, tk = plan["tq"], plan["tk"]
    nk = lp // tk
    return pl.pallas_call(
        functools.partial(_attn_kernel, seq_len, tk, nk, lam_init),
        grid=(b, ATT_HEADS, lp // tq, nk),
        in_specs=[
            pl.BlockSpec((None, tq, ATT_V_DIM), lambda bb, hh, iq, ik: (bb, iq, hh)),
            pl.BlockSpec((None, tk, ATT_V_DIM), lambda bb, hh, iq, ik: (bb, ik, hh)),
            pl.BlockSpec((None, tk, ATT_V_EXT), lambda bb, hh, iq, ik: (bb, ik, hh)),
            pl.BlockSpec((None, 4, ATT_HEAD_DIM), lambda bb, hh, iq, ik: (layer, 0, 0)),
            pl.BlockSpec((None, 1, ATT_V_DIM), lambda bb, hh, iq, ik: (layer, 0, 0)),
        ],
        out_specs=pl.BlockSpec((None, tq, ATT_V_DIM), lambda bb, hh, iq, ik: (bb, iq, hh)),
        out_shape=jax.ShapeDtypeStruct((b, lp, ATT_WIDTH), BF16),
        scratch_shapes=[pltpu.VMEM((tq, V7X_LANES), F32), pltpu.VMEM((tq, V7X_LANES), F32),
                        pltpu.VMEM((tq, ATT_V_EXT), F32), pltpu.VMEM((tq, ATT_V_EXT), F32)],
        compiler_params=_params(("parallel", "parallel", "parallel", "arbitrary")),
        name="attn",
    )(q, k, v, prep["lam"], prep["sub_norm"])


def _gla_kernel(reverse, nchunk, q_ref, k_ref, v_ref, g_ref, o_ref, st_ref):
    c_rows = GLA_CHUNK

    @pl.when(pl.program_id(1) == 0)
    def _():
        st_ref[...] = jnp.zeros_like(st_ref)

    def before(a, b_):
        return (a >= b_) if reverse else (a <= b_)

    ti = lax.broadcasted_iota(jnp.int32, (2 * c_rows, c_rows), 0)
    si = lax.broadcasted_iota(jnp.int32, (2 * c_rows, c_rows), 1)
    t_loc = ti % c_rows
    incl = before(si, t_loc)
    blk = jnp.logical_and(before(si // GLA_SUB, t_loc // GLA_SUB), (si // GLA_SUB) != (t_loc // GLA_SUB))
    cum_mat = jnp.where(ti < c_rows, jnp.where(incl, 1.0, 0.0), jnp.where(blk, 1.0, 0.0)).astype(BF16)

    hr = lax.broadcasted_iota(jnp.int32, (GLA_WIDTH, GLA_WIDTH), 0) // GLA_HEAD_DIM
    hc = lax.broadcasted_iota(jnp.int32, (GLA_WIDTH, GLA_WIDTH), 1) // GLA_HEAD_DIM
    head_mask = hr == hc
    tt = lax.broadcasted_iota(jnp.int32, (c_rows, GLA_WIDTH), 0)
    ss = lax.broadcasted_iota(jnp.int32, (c_rows, GLA_WIDTH), 1) % c_rows
    causal = before(ss, tt)
    row_blk = lax.broadcasted_iota(jnp.int32, (c_rows, 1), 0) // GLA_SUB

    def chunk(ci, carry):
        c = (nchunk - 1 - ci) if reverse else ci
        r0 = pl.multiple_of(c * c_rows, c_rows)
        rows = pl.ds(r0, c_rows)
        q = q_ref[rows, :].astype(F32)
        k = k_ref[rows, :].astype(F32)
        v = v_ref[rows, :]
        g = g_ref[rows, :]

        g1 = g.astype(BF16)
        r1 = g - g1.astype(F32)
        g2 = r1.astype(BF16)
        g3 = (r1 - g2.astype(F32)).astype(BF16)
        cums = _dot(cum_mat, g1) + _dot(cum_mat, g2) + _dot(cum_mat, g3)
        b_incl = cums[:c_rows]
        b_ref = cums[c_rows:]
        b_last = b_incl[0:1] if reverse else b_incl[c_rows - 1:c_rows]

        qs = q * jnp.exp(b_incl - b_ref)
        qe = q * jnp.exp(b_incl)
        kd = k * jnp.exp(b_last - b_incl)

        lhs_parts, rhs_parts = [], []
        for i in range(GLA_NSUB):
            r_i = b_ref[GLA_SUB * i:GLA_SUB * i + 1]
            k_i = jnp.where(before(row_blk, i), k * jnp.exp(jnp.minimum(r_i - b_incl, GLA_EXP_CLAMP)), 0.0)
            k_i4 = jnp.concatenate([k_i] * GLA_HEADS, axis=0)
            rhs_parts.append(jnp.where(head_mask, k_i4, 0.0).astype(BF16))
            lhs_parts.append(jnp.where(row_blk == i, qs, 0.0).astype(BF16))
        lhs = jnp.concatenate(lhs_parts, axis=1)
        rhs = jnp.concatenate(rhs_parts, axis=1)
        a = jnp.where(causal, _dot_nt(lhs, rhs), 0.0).astype(BF16)

        v4 = jnp.where(head_mask, jnp.concatenate([v] * GLA_HEADS, axis=0), jnp.zeros((), BF16))
        st = st_ref[...]
        o_ref[rows, :] = _dot(a, v4) + _dot_nt(qe.astype(BF16), st.astype(BF16))
        upd = _dot_tn(v, kd.astype(BF16))
        st_ref[...] = st * jnp.exp(b_last) + jnp.where(head_mask, upd, 0.0)
        return carry

    lax.fori_loop(0, nchunk, chunk, 0)


def _gla(gq, gk, gv, gdecay, plan, reverse):
    b, lp, _ = gq.shape
    tt = plan["gla_tile"]
    nt = lp // tt
    idx = (lambda bb, i: (bb, nt - 1 - i, 0)) if reverse else (lambda bb, i: (bb, i, 0))
    spec = pl.BlockSpec((None, tt, GLA_WIDTH), idx)
    return pl.pallas_call(
        functools.partial(_gla_kernel, reverse, tt // GLA_CHUNK),
        grid=(b, nt),
        in_specs=[spec, spec, spec, spec],
        out_specs=spec,
        out_shape=jax.ShapeDtypeStruct((b, lp, GLA_WIDTH), F32),
        scratch_shapes=[pltpu.VMEM((GLA_WIDTH, GLA_WIDTH), F32)],
        compiler_params=_params(("parallel", "arbitrary")),
        name="gla_bwd" if reverse else "gla_fwd",
    )(gq, gk, gv, gdecay)


def _post_kernel(seq_len, tm, nt, h_ref, oa_ref, pu_ref, pup_ref, pun_ref, of_ref, ob_ref, gg_ref,
                 pw_ref, ps_ref, gn_ref, bd_ref, wo_ref, o_ref):
    i = pl.program_id(1)
    cur = pu_ref[...]
    prev = jnp.where(i > 0, pup_ref[...], 0.0)
    nxt = jnp.where(i < nt - 1, pun_ref[...], 0.0)
    ext = jnp.concatenate([prev, cur, nxt], axis=0)
    n = tm + 2 * POOL_HALO
    p2 = ext + pltpu.roll(ext, 1, 0)
    p4 = p2 + pltpu.roll(p2, 2, 0)
    p8 = p4 + pltpu.roll(p4, 4, 0)
    p16 = p8 + pltpu.roll(p8, 8, 0)
    sums = (p2, pltpu.roll(p4, n - 1, 0), pltpu.roll(p8, n - 3, 0), pltpu.roll(p16, n - 7, 0))

    pos = i * tm + lax.broadcasted_iota(jnp.int32, (tm, 1), 0)
    grp = lax.broadcasted_iota(jnp.int32, (tm, POOL_WIDTH), 1) // POOL_GROUP_DIM
    wsum = jnp.zeros((tm, POOL_WIDTH), F32)
    cnt = jnp.ones((tm, POOL_WIDTH), F32)
    for gi, win in enumerate(POOL_WINDOWS):
        lo = win // 2
        hi = win - lo - 1
        c = jnp.minimum(pos + hi + 1, seq_len) - jnp.maximum(pos - lo, 0)
        c = jnp.maximum(c, 1).astype(F32)
        sel = grp == gi
        wsum = jnp.where(sel, sums[gi][POOL_HALO:POOL_HALO + tm], wsum)
        cnt = jnp.where(sel, c, cnt)
    pooled = (wsum / cnt - cur).astype(BF16)
    y_pool = (_dot(pooled, pw_ref[...]) * ps_ref[...]).astype(BF16)

    og = of_ref[...] + ob_ref[...]
    og = og * lax.rsqrt(_group_mean_sq(og, bd_ref[...]) + RMS_EPS) * gn_ref[...]
    gate = gg_ref[...]
    y_gla = (og * (gate * jax.nn.sigmoid(gate))).astype(BF16)

    mixed = (_dot(oa_ref[...], wo_ref[0:ATT_WIDTH, :])
             + _dot(y_pool, wo_ref[ATT_WIDTH:ATT_WIDTH + POOL_WIDTH, :])
             + _dot(y_gla, wo_ref[ATT_WIDTH + POOL_WIDTH:MIX_WIDTH, :]))
    o_ref[...] = h_ref[...] + mixed


def _post(h, o_att, pu, o_f, o_b, gg, layer, seq_len, plan, prep):
    b, lp, _ = h.shape
    tm = plan["row_tile"]
    nt = lp // tm
    hb = tm // POOL_HALO
    row = lambda w: pl.BlockSpec((None, tm, w), lambda bb, i: (bb, i, 0))
    par = lambda *shape: pl.BlockSpec((None,) + shape, lambda bb, i: (layer,) + (0,) * len(shape))
    return pl.pallas_call(
        functools.partial(_post_kernel, seq_len, tm, nt),
        grid=(b, nt),
        in_specs=[
            row(D_MODEL),
            row(ATT_WIDTH),
            row(POOL_WIDTH),
            pl.BlockSpec((None, POOL_HALO, POOL_WIDTH), lambda bb, i: (bb, jnp.maximum(i * hb - 1, 0), 0)),
            pl.BlockSpec((None, POOL_HALO, POOL_WIDTH),
                         lambda bb, i: (bb, jnp.minimum((i + 1) * hb, lp // POOL_HALO - 1), 0)),
            row(GLA_WIDTH),
            row(GLA_WIDTH),
            row(GLA_WIDTH),
            par(POOL_WIDTH, POOL_WIDTH),
            par(1, POOL_WIDTH),
            par(1, GLA_WIDTH),
            pl.BlockSpec((V7X_MXU_DIM, V7X_MXU_DIM), lambda bb, i: (0, 0)),
            par(MIX_WIDTH, D_MODEL),
        ],
        out_specs=row(D_MODEL),
        out_shape=jax.ShapeDtypeStruct((b, lp, D_MODEL), F32),
        compiler_params=_params(("parallel", "parallel")),
        name="post",
    )(h, o_att, pu, pu, pu, o_f, o_b, gg, prep["pool_w"], prep["pool_scale"], prep["gla_norm"], prep["bd"],
      prep["w_out"])


def _rope_tables(lp):
    pos = jnp.arange(lp, dtype=F32)
    inv_freq = ROPE_THETA ** (-jnp.arange(0, ATT_HEAD_DIM, 2, dtype=F32) / ATT_HEAD_DIM)
    ang = pos[:, None] * inv_freq[None, :]
    cos, sin = jnp.cos(ang), jnp.sin(ang)
    return jnp.tile(cos, (1, 4)), jnp.tile(jnp.concatenate([-sin, sin], axis=1), (1, 2))


def _prepare(p, padded_lengths):
    eye = jnp.eye(V7X_MXU_DIM // ATT_HEAD_DIM, dtype=F32)
    bd = jnp.kron(eye, jnp.full((ATT_HEAD_DIM, ATT_HEAD_DIM), 1.0 / ATT_HEAD_DIM, F32)).astype(BF16)
    w_in = jnp.pad(p["w_in"], ((0, 0), (0, 0), (0, IN_COLS_PAD - IN_COLS))).astype(BF16)
    gu = jnp.zeros((DEPTH, V7X_LANES, 2 * GLA_WIDTH), F32)
    gu = gu.at[:, :GLA_GATE_RANK, :GLA_WIDTH].set(p["gla_gate_up"][:, 0])
    gu = gu.at[:, GLA_GATE_RANK:2 * GLA_GATE_RANK, GLA_WIDTH:].set(p["gla_gate_up"][:, 1])
    pool_w = jnp.zeros((DEPTH, POOL_WIDTH, POOL_WIDTH), F32)
    for gi in range(len(POOL_WINDOWS)):
        sl = slice(gi * POOL_GROUP_DIM, (gi + 1) * POOL_GROUP_DIM)
        pool_w = pool_w.at[:, sl, sl].set(p["pool_w"][:, gi])
    tables = {lp: _rope_tables(lp) for lp in padded_lengths}
    return dict(
        bd=bd,
        w_in=w_in,
        w_out=p["w_out"].astype(BF16),
        mix_norm=p["mix_norm"][:, None, :],
        q_gain=jnp.tile(p["attn_q_norm"], (1, 2))[:, None, :],
        k_gain=jnp.tile(p["attn_k_norm"], (1, 2))[:, None, :],
        lam=jnp.stack([p["lambda_q1"], p["lambda_k1"], p["lambda_q2"], p["lambda_k2"]], axis=1),
        sub_norm=p["attn_sub_norm"][:, None, :],
        gate_up=gu,
        gate_bias=p["gla_gate_bias"].reshape(DEPTH, 1, 2 * GLA_WIDTH),
        pool_w=pool_w.astype(BF16),
        pool_scale=p["pool_scale"][:, None, :],
        gla_norm=jnp.tile(p["gla_out_norm"], (1, GLA_HEADS))[:, None, :],
        ffn1_norm=p["ffn1_norm"][:, None, :],
        ffn2_norm=p["ffn2_norm"][:, None, :],
        ffn1_w_in=p["ffn1_w_in"].astype(BF16),
        ffn1_w_out=p["ffn1_w_out"].astype(BF16),
        ffn2_w_in=p["ffn2_w_in"].astype(BF16),
        ffn2_w_out=p["ffn2_w_out"].astype(BF16),
        cos={lp: t[0] for lp, t in tables.items()},
        sin={lp: t[1] for lp, t in tables.items()},
    )


def _trunk(x, meta_tokens, prep):
    b, s, _ = x.shape
    seq_len = s + N_META
    plan = _plan(seq_len)
    lp = plan["lp"]
    meta = jnp.broadcast_to(meta_tokens.astype(x.dtype)[None], (b, N_META, D_MODEL))
    h = jnp.concatenate([meta, x, jnp.zeros((b, lp - seq_len, D_MODEL), x.dtype)], axis=1)
    for layer in range(DEPTH):
        lam_init = 0.8 - 0.6 * math.exp(-0.3 * layer)
        h = _ffn(h.reshape(b * lp, D_MODEL), layer, prep["ffn1_norm"], prep["ffn1_w_in"],
                 prep["ffn1_w_out"]).reshape(b, lp, D_MODEL)
        q, k, v, pu, gq, gk, gv, gg, gf, gb = _proj(h, layer, seq_len, plan, prep)
        o_att = _attn(q, k, v, layer, seq_len, plan, prep, lam_init)
        o_f = _gla(gq, gk, gv, gf, plan, reverse=False)
        o_b = _gla(gq, gk, gv, gb, plan, reverse=True)
        h = _post(h, o_att, pu, o_f, o_b, gg, layer, seq_len, plan, prep)
        h = _ffn(h.reshape(b * lp, D_MODEL), layer, prep["ffn2_norm"], prep["ffn2_w_in"],
                 prep["ffn2_w_out"]).reshape(b, lp, D_MODEL)
    return h[:, N_META:seq_len]


def kernel(x_prompt, x_sample, meta_tokens, ffn1_norm, ffn1_w_in, ffn1_w_out, mix_norm, w_in, w_out, attn_q_norm,
           attn_k_norm, lambda_q1, lambda_k1, lambda_q2, lambda_k2, attn_sub_norm, pool_w, pool_scale, gla_gate_up,
           gla_gate_bias, gla_out_norm, ffn2_norm, ffn2_w_in, ffn2_w_out):
    p = dict(ffn1_norm=ffn1_norm, ffn1_w_in=ffn1_w_in, ffn1_w_out=ffn1_w_out, mix_norm=mix_norm, w_in=w_in,
             w_out=w_out, attn_q_norm=attn_q_norm, attn_k_norm=attn_k_norm, lambda_q1=lambda_q1,
             lambda_k1=lambda_k1, lambda_q2=lambda_q2, lambda_k2=lambda_k2, attn_sub_norm=attn_sub_norm,
             pool_w=pool_w, pool_scale=pool_scale, gla_gate_up=gla_gate_up, gla_gate_bias=gla_gate_bias,
             gla_out_norm=gla_out_norm, ffn2_norm=ffn2_norm, ffn2_w_in=ffn2_w_in, ffn2_w_out=ffn2_w_out)
    lengths = {_plan(x.shape[1] + N_META)["lp"] for x in (x_prompt, x_sample)}
    prep = _prepare(p, lengths)
    return (_trunk(x_prompt, meta_tokens, prep), _trunk(x_sample, meta_tokens, prep))
```

```python
import functools
import math

import jax
import jax.numpy as jnp
from jax import lax
from jax.experimental import pallas as pl
from jax.experimental.pallas import tpu as pltpu

F32 = jnp.float32
BF16 = jnp.bfloat16

D_MODEL = 1024
D_FF = 2816
DEPTH = 4
N_META = 16
RMS_EPS = 1e-6
ROPE_THETA = 10000.0

ATT_HEADS = 4
ATT_HEAD_DIM = 64
ATT_V_DIM = 2 * ATT_HEAD_DIM
ATT_WIDTH = ATT_HEADS * ATT_V_DIM
ATT_V_EXT = 2 * ATT_V_DIM

POOL_WINDOWS = (2, 4, 8, 16)
POOL_GROUP_DIM = 64
POOL_WIDTH = len(POOL_WINDOWS) * POOL_GROUP_DIM
POOL_HALO = 8

GLA_HEADS = 4
GLA_HEAD_DIM = 64
GLA_WIDTH = GLA_HEADS * GLA_HEAD_DIM
GLA_GATE_RANK = 16
GLA_GATE_TEMP = 16.0
GLA_CHUNK = 64
GLA_SUB = 16
GLA_NSUB = GLA_CHUNK // GLA_SUB
GLA_EXP_CLAMP = 80.0

MIX_WIDTH = ATT_WIDTH + POOL_WIDTH + GLA_WIDTH
IN_COLS = 4 * ATT_WIDTH - ATT_WIDTH + POOL_WIDTH + 4 * GLA_WIDTH + 2 * GLA_GATE_RANK
GATE_COL0 = IN_COLS - 2 * GLA_GATE_RANK
IN_COLS_PAD = GATE_COL0 + 128

V7X_LANES = 128
V7X_MXU_DIM = 256
SEQ_ALIGN = 128
SHORT_SEQ_ROWS = 2304
ATT_TILE_CAP = 1280
ATT_KV_TILE_LONG = 1280
ROW_TILE_CAP = 704
ATT_ROW_BLOCK_LONG = 128
ATT_ROW_BLOCK_SHORT = 384
GLA_TILE_CAP = 1280
FFN_TILE_CAP = 704
VMEM_LIMIT_BYTES = 56 * 1024 * 1024
FF_TILE = 256
LOG2E = 1.4426950408889634


def _divisor_tile(n, cap, mult):
    best = None
    for t in range(mult, min(n, cap) + 1, mult):
        if n % t == 0:
            best = t
    assert best is not None, (n, cap, mult)
    return best


def _plan(seq_len):
    if seq_len <= SHORT_SEQ_ROWS:
        lp = -(-seq_len // SEQ_ALIGN) * SEQ_ALIGN
        tk = lp
        row_block_cap = ATT_ROW_BLOCK_SHORT
    else:
        tk = ATT_KV_TILE_LONG
        lp = -(-seq_len // tk) * tk
        row_block_cap = ATT_ROW_BLOCK_LONG
    tq = _divisor_tile(lp, ATT_TILE_CAP, 16)
    return dict(
        lp=lp,
        row_tile=_divisor_tile(lp, ROW_TILE_CAP, 16),
        tq=tq,
        tk=tk,
        rb=_divisor_tile(tq, row_block_cap, 16),
        gla_tile=_divisor_tile(lp, GLA_TILE_CAP, GLA_CHUNK),
    )


def _params(sem):
    return pltpu.CompilerParams(dimension_semantics=sem, vmem_limit_bytes=VMEM_LIMIT_BYTES)


def _dot(a, b):
    return jnp.dot(a, b, preferred_element_type=F32)


def _dot_nt(a, b):
    return lax.dot_general(a, b, (((1,), (1,)), ((), ())), preferred_element_type=F32)


def _dot_tn(a, b):
    return lax.dot_general(a, b, (((0,), (0,)), ((), ())), preferred_element_type=F32)


def _split2(x):
    hi = x.astype(BF16)
    lo = (x - hi.astype(F32)).astype(BF16)
    return hi, lo


def _group_mean_sq(x, bd):
    hi, lo = _split2(x * x)
    outs = []
    for c in range(x.shape[1] // V7X_MXU_DIM):
        sl = slice(V7X_MXU_DIM * c, V7X_MXU_DIM * (c + 1))
        outs.append(_dot(hi[:, sl], bd) + _dot(lo[:, sl], bd))
    return outs[0] if len(outs) == 1 else jnp.concatenate(outs, axis=1)


def _ffn_kernel(x_ref, g_ref, wi_ref, wo_ref, o_ref):
    x = x_ref[...]
    ms = jnp.mean(x * x, axis=-1, keepdims=True)
    xn = (x * lax.rsqrt(ms + RMS_EPS) * g_ref[...]).astype(BF16)
    acc = None
    for c0 in range(0, D_FF, FF_TILE):
        g = _dot(xn, wi_ref[:, c0:c0 + FF_TILE])
        u = _dot(xn, wi_ref[:, D_FF + c0:D_FF + c0 + FF_TILE])
        a = (g * jax.nn.sigmoid(g) * u).astype(BF16)
        d = _dot(a, wo_ref[c0:c0 + FF_TILE, :])
        acc = d if acc is None else acc + d
    o_ref[...] = x + 0.5 * acc


def _ffn(h2d, layer, norm, w_in, w_out):
    rows = h2d.shape[0]
    tm = _divisor_tile(rows, FFN_TILE_CAP, 16)
    resident = lambda *shape: pl.BlockSpec((None,) + shape, lambda i: (layer,) + (0,) * len(shape),
                                           pipeline_mode=pl.Buffered(1))
    return pl.pallas_call(
        _ffn_kernel,
        grid=(rows // tm,),
        in_specs=[
            pl.BlockSpec((tm, D_MODEL), lambda i: (i, 0)),
            pl.BlockSpec((None, 1, D_MODEL), lambda i: (layer, 0, 0)),
            resident(D_MODEL, 2 * D_FF),
            resident(D_FF, D_MODEL),
        ],
        out_specs=pl.BlockSpec((tm, D_MODEL), lambda i: (i, 0)),
        out_shape=jax.ShapeDtypeStruct((rows, D_MODEL), F32),
        compiler_params=_params(("parallel",)),
        name="ffn",
    )(h2d, norm, w_in, w_out)


def _rope128(x, cos, sin_signed, first_half):
    partner = jnp.where(first_half, pltpu.roll(x, 96, 1), pltpu.roll(x, 32, 1))
    return x * cos + partner * sin_signed


def _proj_kernel(seq_len, tm, h_ref, g_ref, w_ref, cos_ref, sin_ref, qg_ref, kg_ref, bd_ref, gu_ref, gbias_ref,
                 q_ref, k_ref, v_ref, pu_ref, gq_ref, gk_ref, gv_ref, gg_ref, gf_ref, gb_ref):
    i = pl.program_id(1)
    x = h_ref[...]
    pos = i * tm + lax.broadcasted_iota(jnp.int32, (tm, 1), 0)
    valid = pos < seq_len
    ms = jnp.mean(x * x, axis=-1, keepdims=True)
    xn = jnp.where(valid, x * lax.rsqrt(ms + RMS_EPS) * g_ref[...], 0.0).astype(BF16)

    bd = bd_ref[...]
    cos = cos_ref[...]
    sin = sin_ref[...]
    lane = lax.broadcasted_iota(jnp.int32, (tm, V7X_LANES), 1)
    first_half = (lane % ATT_HEAD_DIM) < (ATT_HEAD_DIM // 2)

    def qk_path(col0, gain_ref, out_ref, out_scale):
        z = _dot(xn, w_ref[:, col0:col0 + ATT_WIDTH])
        zn = z * lax.rsqrt(_group_mean_sq(z, bd) + RMS_EPS)
        gain = gain_ref[...]
        for c in range(ATT_WIDTH // V7X_LANES):
            sl = slice(V7X_LANES * c, V7X_LANES * (c + 1))
            r = _rope128(zn[:, sl] * gain, cos, sin, first_half)
            out_ref[:, sl] = (r * out_scale).astype(BF16)

    qk_path(0, qg_ref, q_ref, ATT_HEAD_DIM ** -0.5 * LOG2E)
    qk_path(ATT_WIDTH, kg_ref, k_ref, 1.0)

    zv = _dot(xn, w_ref[:, 2 * ATT_WIDTH:3 * ATT_WIDTH])
    ones_col = jnp.where(jnp.logical_and(lane == 0, valid), 1.0, 0.0).astype(BF16)
    for hh in range(ATT_HEADS):
        v_ref[:, ATT_V_EXT * hh:ATT_V_EXT * hh + ATT_V_DIM] = zv[:, ATT_V_DIM * hh:ATT_V_DIM * (hh + 1)].astype(BF16)
        v_ref[:, ATT_V_EXT * hh + ATT_V_DIM:ATT_V_EXT * (hh + 1)] = ones_col

    c0 = 3 * ATT_WIDTH
    pu_ref[...] = _dot(xn, w_ref[:, c0:c0 + POOL_WIDTH])
    c0 += POOL_WIDTH
    gq_ref[...] = (_dot(xn, w_ref[:, c0:c0 + GLA_WIDTH]) * (GLA_HEAD_DIM ** -0.5)).astype(BF16)
    c0 += GLA_WIDTH
    gk_ref[...] = _dot(xn, w_ref[:, c0:c0 + GLA_WIDTH]).astype(BF16)
    c0 += GLA_WIDTH
    gv_ref[...] = _dot(xn, w_ref[:, c0:c0 + GLA_WIDTH]).astype(BF16)
    c0 += GLA_WIDTH
    gg_ref[...] = _dot(xn, w_ref[:, c0:c0 + GLA_WIDTH])

    glr = _dot(xn, w_ref[:, GATE_COL0:IN_COLS_PAD])
    glr_hi, glr_lo = _split2(glr)
    gu_hi, gu_lo = _split2(gu_ref[...])
    logit = _dot(glr_hi, gu_hi) + _dot(glr_lo, gu_hi) + _dot(glr_hi, gu_lo) + gbias_ref[...]
    log_sig = jnp.minimum(logit, 0.0) - jnp.log(1.0 + jnp.exp(-jnp.abs(logit)))
    log_a = log_sig * (1.0 / GLA_GATE_TEMP)
    gf_ref[...] = log_a[:, :GLA_WIDTH]
    gb_ref[...] = log_a[:, GLA_WIDTH:]


def _proj(h, layer, seq_len, plan, prep):
    b, lp, _ = h.shape
    tm = plan["row_tile"]
    row = lambda w: pl.BlockSpec((None, tm, w), lambda bb, i: (bb, i, 0))
    par = lambda *shape: pl.BlockSpec((None,) + shape, lambda bb, i: (layer,) + (0,) * len(shape))
    out_widths = [(ATT_WIDTH, BF16), (ATT_WIDTH, BF16), (ATT_HEADS * ATT_V_EXT, BF16), (POOL_WIDTH, F32),
                  (GLA_WIDTH, BF16), (GLA_WIDTH, BF16), (GLA_WIDTH, BF16), (GLA_WIDTH, F32),
                  (GLA_WIDTH, F32), (GLA_WIDTH, F32)]
    return pl.pallas_call(
        functools.partial(_proj_kernel, seq_len, tm),
        grid=(b, lp // tm),
        in_specs=[
            row(D_MODEL),
            par(1, D_MODEL),
            par(D_MODEL, IN_COLS_PAD),
            pl.BlockSpec((tm, V7X_LANES), lambda bb, i: (i, 0)),
            pl.BlockSpec((tm, V7X_LANES), lambda bb, i: (i, 0)),
            par(1, V7X_LANES),
            par(1, V7X_LANES),
            pl.BlockSpec((V7X_MXU_DIM, V7X_MXU_DIM), lambda bb, i: (0, 0)),
            par(V7X_LANES, 2 * GLA_WIDTH),
            par(1, 2 * GLA_WIDTH),
        ],
        out_specs=[row(w) for w, _ in out_widths],
        out_shape=[jax.ShapeDtypeStruct((b, lp, w), dt) for w, dt in out_widths],
        compiler_params=_params(("parallel", "parallel")),
        name="proj",
    )(h, prep["mix_norm"], prep["w_in"], prep["cos"][lp], prep["sin"][lp], prep["q_gain"], prep["k_gain"],
      prep["bd"], prep["gate_up"], prep["gate_bias"])


def _attn_kernel(seq_len, tq, tk, rb, nk, lam_init, q_ref, k_ref, v_ref, lam_ref, sn_ref, o_ref,
                 m0_ref, m1_ref, acc0_ref, acc1_ref):
    ik = pl.program_id(3)

    @pl.when(ik == 0)
    def _():
        m0_ref[...] = jnp.full_like(m0_ref, -jnp.inf)
        m1_ref[...] = jnp.full_like(m1_ref, -jnp.inf)
        acc0_ref[...] = jnp.zeros_like(acc0_ref)
        acc1_ref[...] = jnp.zeros_like(acc1_ref)

    def step(mask_tail):
        k = k_ref[...]
        v = v_ref[...]
        lane = lax.broadcasted_iota(jnp.int32, k.shape, 1)
        halves = (jnp.where(lane < ATT_HEAD_DIM, k, jnp.zeros_like(k)),
                  jnp.where(lane >= ATT_HEAD_DIM, k, jnp.zeros_like(k)))
        for r0 in range(0, tq, rb):
            rows = slice(r0, r0 + rb)
            q = q_ref[rows, :]
            for kc, m_ref, acc_ref in ((halves[0], m0_ref, acc0_ref), (halves[1], m1_ref, acc1_ref)):
                s = _dot_nt(q, kc)
                if mask_tail:
                    col = ik * tk + lax.broadcasted_iota(jnp.int32, s.shape, 1)
                    s = jnp.where(col < seq_len, s, -jnp.inf)
                m_prev = m_ref[rows, :]
                m_new = jnp.maximum(m_prev, jnp.max(s, axis=-1, keepdims=True))
                alpha = jnp.exp2(m_prev - m_new)
                p = jnp.exp2(s - m_new[:, :1])
                acc_ref[rows, :] = alpha[:, :1] * acc_ref[rows, :] + _dot(p.astype(BF16), v)
                m_ref[rows, :] = m_new

    if nk * tk == seq_len:
        step(False)
    elif nk == 1:
        step(True)
    else:
        pl.when(ik < nk - 1)(lambda: step(False))
        pl.when(ik == nk - 1)(lambda: step(True))

    @pl.when(ik == nk - 1)
    def _():
        lp = lam_ref[...]
        a1 = jnp.sum(lp[0:1] * lp[1:2], axis=-1, keepdims=True)
        a2 = jnp.sum(lp[2:3] * lp[3:4], axis=-1, keepdims=True)
        lam = jnp.exp(a1) - jnp.exp(a2) + lam_init
        acc0 = acc0_ref[...]
        acc1 = acc1_ref[...]
        o = (acc0[:, :ATT_V_DIM] / acc0[:, ATT_V_DIM:ATT_V_DIM + 1]
             - lam * (acc1[:, :ATT_V_DIM] / acc1[:, ATT_V_DIM:ATT_V_DIM + 1]))
        ms = jnp.mean(o * o, axis=-1, keepdims=True)
        o = o * lax.rsqrt(ms + RMS_EPS) * sn_ref[...] * (1.0 - lam_init)
        o_ref[...] = o.astype(BF16)


def _attn(q, k, v, layer, seq_len, plan, prep, lam_init):
    b, lp, _ = q.shape
    tq, tk = plan["tq"], plan["tk"]
    nk = lp // tk
    return pl.pallas_call(
        functools.partial(_attn_kernel, seq_len, tq, tk, plan["rb"], nk, lam_init),
        grid=(b, ATT_HEADS, lp // tq, nk),
        in_specs=[
            pl.BlockSpec((None, tq, ATT_V_DIM), lambda bb, hh, iq, ik: (bb, iq, hh)),
            pl.BlockSpec((None, tk, ATT_V_DIM), lambda bb, hh, iq, ik: (bb, ik, hh)),
            pl.BlockSpec((None, tk, ATT_V_EXT), lambda bb, hh, iq, ik: (bb, ik, hh)),
            pl.BlockSpec((None, 4, ATT_HEAD_DIM), lambda bb, hh, iq, ik: (layer, 0, 0)),
            pl.BlockSpec((None, 1, ATT_V_DIM), lambda bb, hh, iq, ik: (layer, 0, 0)),
        ],
        out_specs=pl.BlockSpec((None, tq, ATT_V_DIM), lambda bb, hh, iq, ik: (bb, iq, hh)),
        out_shape=jax.ShapeDtypeStruct((b, lp, ATT_WIDTH), BF16),
        scratch_shapes=[pltpu.VMEM((tq, V7X_LANES), F32), pltpu.VMEM((tq, V7X_LANES), F32),
                        pltpu.VMEM((tq, ATT_V_EXT), F32), pltpu.VMEM((tq, ATT_V_EXT), F32)],
        compiler_params=_params(("parallel", "parallel", "parallel", "arbitrary")),
        name="attn",
    )(q, k, v, prep["lam"], prep["sub_norm"])


def _gla_kernel(reverse, nchunk, q_ref, k_ref, v_ref, g_ref, o_ref, st_ref):
    c_rows = GLA_CHUNK

    @pl.when(pl.program_id(1) == 0)
    def _():
        st_ref[...] = jnp.zeros_like(st_ref)

    def before(a, b_):
        return (a >= b_) if reverse else (a <= b_)

    ti = lax.broadcasted_iota(jnp.int32, (2 * c_rows, c_rows), 0)
    si = lax.broadcasted_iota(jnp.int32, (2 * c_rows, c_rows), 1)
    t_loc = ti % c_rows
    incl = before(si, t_loc)
    blk = jnp.logical_and(before(si // GLA_SUB, t_loc // GLA_SUB), (si // GLA_SUB) != (t_loc // GLA_SUB))
    cum_mat = jnp.where(ti < c_rows, jnp.where(incl, 1.0, 0.0), jnp.where(blk, 1.0, 0.0)).astype(BF16)

    hr = lax.broadcasted_iota(jnp.int32, (GLA_WIDTH, GLA_WIDTH), 0) // GLA_HEAD_DIM
    hc = lax.broadcasted_iota(jnp.int32, (GLA_WIDTH, GLA_WIDTH), 1) // GLA_HEAD_DIM
    head_mask = hr == hc
    tt = lax.broadcasted_iota(jnp.int32, (c_rows, GLA_WIDTH), 0)
    ss = lax.broadcasted_iota(jnp.int32, (c_rows, GLA_WIDTH), 1) % c_rows
    causal = before(ss, tt)
    row_blk = lax.broadcasted_iota(jnp.int32, (c_rows, 1), 0) // GLA_SUB

    def chunk(ci, carry):
        c = (nchunk - 1 - ci) if reverse else ci
        r0 = pl.multiple_of(c * c_rows, c_rows)
        rows = pl.ds(r0, c_rows)
        q = q_ref[rows, :].astype(F32)
        k = k_ref[rows, :].astype(F32)
        v = v_ref[rows, :]
        g = g_ref[rows, :]

        g1 = g.astype(BF16)
        r1 = g - g1.astype(F32)
        g2 = r1.astype(BF16)
        g3 = (r1 - g2.astype(F32)).astype(BF16)
        cums = _dot(cum_mat, g1) + _dot(cum_mat, g2) + _dot(cum_mat, g3)
        b_incl = cums[:c_rows]
        b_ref = cums[c_rows:]
        b_last = b_incl[0:1] if reverse else b_incl[c_rows - 1:c_rows]

        qs = q * jnp.exp(b_incl - b_ref)
        qe = q * jnp.exp(b_incl)
        kd = k * jnp.exp(b_last - b_incl)

        lhs_parts, rhs_parts = [], []
        for i in range(GLA_NSUB):
            r_i = b_ref[GLA_SUB * i:GLA_SUB * i + 1]
            k_i = jnp.where(before(row_blk, i), k * jnp.exp(jnp.minimum(r_i - b_incl, GLA_EXP_CLAMP)), 0.0)
            k_i4 = jnp.concatenate([k_i] * GLA_HEADS, axis=0)
            rhs_parts.append(jnp.where(head_mask, k_i4, 0.0).astype(BF16))
            lhs_parts.append(jnp.where(row_blk == i, qs, 0.0).astype(BF16))
        lhs = jnp.concatenate(lhs_parts, axis=1)
        rhs = jnp.concatenate(rhs_parts, axis=1)
        a = jnp.where(causal, _dot_nt(lhs, rhs), 0.0).astype(BF16)

        v4 = jnp.where(head_mask, jnp.concatenate([v] * GLA_HEADS, axis=0), jnp.zeros((), BF16))
        st = st_ref[...]
        o_ref[rows, :] = _dot(a, v4) + _dot_nt(qe.astype(BF16), st.astype(BF16))
        upd = _dot_tn(v, kd.astype(BF16))
        st_ref[...] = st * jnp.exp(b_last) + jnp.where(head_mask, upd, 0.0)
        return carry

    lax.fori_loop(0, nchunk, chunk, 0)


def _gla(gq, gk, gv, gdecay, plan, reverse):
    b, lp, _ = gq.shape
    tt = plan["gla_tile"]
    nt = lp // tt
    idx = (lambda bb, i: (bb, nt - 1 - i, 0)) if reverse else (lambda bb, i: (bb, i, 0))
    spec = pl.BlockSpec((None, tt, GLA_WIDTH), idx)
    return pl.pallas_call(
        functools.partial(_gla_kernel, reverse, tt // GLA_CHUNK),
        grid=(b, nt),
        in_specs=[spec, spec, spec, spec],
        out_specs=spec,
        out_shape=jax.ShapeDtypeStruct((b, lp, GLA_WIDTH), F32),
        scratch_shapes=[pltpu.VMEM((GLA_WIDTH, GLA_WIDTH), F32)],
        compiler_params=_params(("parallel", "arbitrary")),
        name="gla_bwd" if reverse else "gla_fwd",
    )(gq, gk, gv, gdecay)


def _post_kernel(seq_len, tm, nt, h_ref, oa_ref, pu_ref, pup_ref, pun_ref, of_ref, ob_ref, gg_ref,
                 pw_ref, ps_ref, gn_ref, bd_ref, wo_ref, o_ref):
    i = pl.program_id(1)
    cur = pu_ref[...]
    prev = jnp.where(i > 0, pup_ref[...], 0.0)
    nxt = jnp.where(i < nt - 1, pun_ref[...], 0.0)
    ext = jnp.concatenate([prev, cur, nxt], axis=0)
    n = tm + 2 * POOL_HALO
    p2 = ext + pltpu.roll(ext, 1, 0)
    p4 = p2 + pltpu.roll(p2, 2, 0)
    p8 = p4 + pltpu.roll(p4, 4, 0)
    p16 = p8 + pltpu.roll(p8, 8, 0)
    sums = (p2, pltpu.roll(p4, n - 1, 0), pltpu.roll(p8, n - 3, 0), pltpu.roll(p16, n - 7, 0))

    pos = i * tm + lax.broadcasted_iota(jnp.int32, (tm, 1), 0)
    grp = lax.broadcasted_iota(jnp.int32, (tm, POOL_WIDTH), 1) // POOL_GROUP_DIM
    wsum = jnp.zeros((tm, POOL_WIDTH), F32)
    cnt = jnp.ones((tm, POOL_WIDTH), F32)
    for gi, win in enumerate(POOL_WINDOWS):
        lo = win // 2
        hi = win - lo - 1
        c = jnp.minimum(pos + hi + 1, seq_len) - jnp.maximum(pos - lo, 0)
        c = jnp.maximum(c, 1).astype(F32)
        sel = grp == gi
        wsum = jnp.where(sel, sums[gi][POOL_HALO:POOL_HALO + tm], wsum)
        cnt = jnp.where(sel, c, cnt)
    pooled = (wsum / cnt - cur).astype(BF16)
    y_pool = (_dot(pooled, pw_ref[...]) * ps_ref[...]).astype(BF16)

    og = of_ref[...] + ob_ref[...]
    og = og * lax.rsqrt(_group_mean_sq(og, bd_ref[...]) + RMS_EPS) * gn_ref[...]
    gate = gg_ref[...]
    y_gla = (og * (gate * jax.nn.sigmoid(gate))).astype(BF16)

    mixed = (_dot(oa_ref[...], wo_ref[0:ATT_WIDTH, :])
             + _dot(y_pool, wo_ref[ATT_WIDTH:ATT_WIDTH + POOL_WIDTH, :])
             + _dot(y_gla, wo_ref[ATT_WIDTH + POOL_WIDTH:MIX_WIDTH, :]))
    o_ref[...] = h_ref[...] + mixed


def _post(h, o_att, pu, o_f, o_b, gg, layer, seq_len, plan, prep):
    b, lp, _ = h.shape
    tm = plan["row_tile"]
    nt = lp // tm
    hb = tm // POOL_HALO
    row = lambda w: pl.BlockSpec((None, tm, w), lambda bb, i: (bb, i, 0))
    par = lambda *shape: pl.BlockSpec((None,) + shape, lambda bb, i: (layer,) + (0,) * len(shape))
    return pl.pallas_call(
        functools.partial(_post_kernel, seq_len, tm, nt),
        grid=(b, nt),
        in_specs=[
            row(D_MODEL),
            row(ATT_WIDTH),
            row(POOL_WIDTH),
            pl.BlockSpec((None, POOL_HALO, POOL_WIDTH), lambda bb, i: (bb, jnp.maximum(i * hb - 1, 0), 0)),
            pl.BlockSpec((None, POOL_HALO, POOL_WIDTH),
                         lambda bb, i: (bb, jnp.minimum((i + 1) * hb, lp // POOL_HALO - 1), 0)),
            row(GLA_WIDTH),
            row(GLA_WIDTH),
            row(GLA_WIDTH),
            par(POOL_WIDTH, POOL_WIDTH),
            par(1, POOL_WIDTH),
            par(1, GLA_WIDTH),
            pl.BlockSpec((V7X_MXU_DIM, V7X_MXU_DIM), lambda bb, i: (0, 0)),
            par(MIX_WIDTH, D_MODEL),
        ],
        out_specs=row(D_MODEL),
        out_shape=jax.ShapeDtypeStruct((b, lp, D_MODEL), F32),
        compiler_params=_params(("parallel", "parallel")),
        name="post",
    )(h, o_att, pu, pu, pu, o_f, o_b, gg, prep["pool_w"], prep["pool_scale"], prep["gla_norm"], prep["bd"],
      prep["w_out"])


def _rope_tables(lp):
    pos = jnp.arange(lp, dtype=F32)
    inv_freq = ROPE_THETA ** (-jnp.arange(0, ATT_HEAD_DIM, 2, dtype=F32) / ATT_HEAD_DIM)
    ang = pos[:, None] * inv_freq[None, :]
    cos, sin = jnp.cos(ang), jnp.sin(ang)
    return jnp.tile(cos, (1, 4)), jnp.tile(jnp.concatenate([-sin, sin], axis=1), (1, 2))


def _prepare(p, padded_lengths):
    eye = jnp.eye(V7X_MXU_DIM // ATT_HEAD_DIM, dtype=F32)
    bd = jnp.kron(eye, jnp.full((ATT_HEAD_DIM, ATT_HEAD_DIM), 1.0 / ATT_HEAD_DIM, F32)).astype(BF16)
    w_in = jnp.pad(p["w_in"], ((0, 0), (0, 0), (0, IN_COLS_PAD - IN_COLS))).astype(BF16)
    gu = jnp.zeros((DEPTH, V7X_LANES, 2 * GLA_WIDTH), F32)
    gu = gu.at[:, :GLA_GATE_RANK, :GLA_WIDTH].set(p["gla_gate_up"][:, 0])
    gu = gu.at[:, GLA_GATE_RANK:2 * GLA_GATE_RANK, GLA_WIDTH:].set(p["gla_gate_up"][:, 1])
    pool_w = jnp.zeros((DEPTH, POOL_WIDTH, POOL_WIDTH), F32)
    for gi in range(len(POOL_WINDOWS)):
        sl = slice(gi * POOL_GROUP_DIM, (gi + 1) * POOL_GROUP_DIM)
        pool_w = pool_w.at[:, sl, sl].set(p["pool_w"][:, gi])
    tables = {lp: _rope_tables(lp) for lp in padded_lengths}
    return dict(
        bd=bd,
        w_in=w_in,
        w_out=p["w_out"].astype(BF16),
        mix_norm=p["mix_norm"][:, None, :],
        q_gain=jnp.tile(p["attn_q_norm"], (1, 2))[:, None, :],
        k_gain=jnp.tile(p["attn_k_norm"], (1, 2))[:, None, :],
        lam=jnp.stack([p["lambda_q1"], p["lambda_k1"], p["lambda_q2"], p["lambda_k2"]], axis=1),
        sub_norm=p["attn_sub_norm"][:, None, :],
        gate_up=gu,
        gate_bias=p["gla_gate_bias"].reshape(DEPTH, 1, 2 * GLA_WIDTH),
        pool_w=pool_w.astype(BF16),
        pool_scale=p["pool_scale"][:, None, :],
        gla_norm=jnp.tile(p["gla_out_norm"], (1, GLA_HEADS))[:, None, :],
        ffn1_norm=p["ffn1_norm"][:, None, :],
        ffn2_norm=p["ffn2_norm"][:, None, :],
        ffn1_w_in=p["ffn1_w_in"].astype(BF16),
        ffn1_w_out=p["ffn1_w_out"].astype(BF16),
        ffn2_w_in=p["ffn2_w_in"].astype(BF16),
        ffn2_w_out=p["ffn2_w_out"].astype(BF16),
        cos={lp: t[0] for lp, t in tables.items()},
        sin={lp: t[1] for lp, t in tables.items()},
    )


def _trunk(x, meta_tokens, prep):
    b, s, _ = x.shape
    seq_len = s + N_META
    plan = _plan(seq_len)
    lp = plan["lp"]
    meta = jnp.broadcast_to(meta_tokens.astype(x.dtype)[None], (b, N_META, D_MODEL))
    h = jnp.concatenate([meta, x, jnp.zeros((b, lp - seq_len, D_MODEL), x.dtype)], axis=1)
    for layer in range(DEPTH):
        lam_init = 0.8 - 0.6 * math.exp(-0.3 * layer)
        h = _ffn(h.reshape(b * lp, D_MODEL), layer, prep["ffn1_norm"], prep["ffn1_w_in"],
                 prep["ffn1_w_out"]).reshape(b, lp, D_MODEL)
        q, k, v, pu, gq, gk, gv, gg, gf, gb = _proj(h, layer, seq_len, plan, prep)
        o_att = _attn(q, k, v, layer, seq_len, plan, prep, lam_init)
        o_f = _gla(gq, gk, gv, gf, plan, reverse=False)
        o_b = _gla(gq, gk, gv, gb, plan, reverse=True)
        h = _post(h, o_att, pu, o_f, o_b, gg, layer, seq_len, plan, prep)
        h = _ffn(h.reshape(b * lp, D_MODEL), layer, prep["ffn2_norm"], prep["ffn2_w_in"],
                 prep["ffn2_w_out"]).reshape(b, lp, D_MODEL)
    return h[:, N_META:seq_len]


def kernel(x_prompt, x_sample, meta_tokens, ffn1_norm, ffn1_w_in, ffn1_w_out, mix_norm, w_in, w_out, attn_q_norm,
           attn_k_norm, lambda_q1, lambda_k1, lambda_q2, lambda_k2, attn_sub_norm, pool_w, pool_scale, gla_gate_up,
           gla_gate_bias, gla_out_norm, ffn2_norm, ffn2_w_in, ffn2_w_out):
    p = dict(ffn1_norm=ffn1_norm, ffn1_w_in=ffn1_w_in, ffn1_w_out=ffn1_w_out, mix_norm=mix_norm, w_in=w_in,
             w_out=w_out, attn_q_norm=attn_q_norm, attn_k_norm=attn_k_norm, lambda_q1=lambda_q1,
             lambda_k1=lambda_k1, lambda_q2=lambda_q2, lambda_k2=lambda_k2, attn_sub_norm=attn_sub_norm,
             pool_w=pool_w, pool_scale=pool_scale, gla_gate_up=gla_gate_up, gla_gate_bias=gla_gate_bias,
             gla_out_norm=gla_out_norm, ffn2_norm=ffn2_norm, ffn2_w_in=ffn2_w_in, ffn2_w_out=ffn2_w_out)
    lengths = {_plan(x.shape[1] + N_META)["lp"] for x in (x_prompt, x_sample)}
    prep = _prepare(p, lengths)
    return (_trunk(x_prompt, meta_tokens, prep), _trunk(x_sample, meta_tokens, prep))
```

```python
import functools
import math

import jax
import jax.numpy as jnp
from jax import lax
from jax.experimental import pallas as pl
from jax.experimental.pallas import tpu as pltpu

F32 = jnp.float32
BF16 = jnp.bfloat16

D_MODEL = 1024
D_FF = 2816
DEPTH = 4
N_META = 16
RMS_EPS = 1e-6
ROPE_THETA = 10000.0

ATT_HEADS = 4
ATT_HEAD_DIM = 64
ATT_V_DIM = 2 * ATT_HEAD_DIM
ATT_WIDTH = ATT_HEADS * ATT_V_DIM
ATT_ACC_ROWS = ATT_V_DIM + 16

POOL_WINDOWS = (2, 4, 8, 16)
POOL_GROUP_DIM = 64
POOL_WIDTH = len(POOL_WINDOWS) * POOL_GROUP_DIM
POOL_HALO = 8

GLA_HEADS = 4
GLA_HEAD_DIM = 64
GLA_WIDTH = GLA_HEADS * GLA_HEAD_DIM
GLA_GATE_RANK = 16
GLA_GATE_TEMP = 16.0
GLA_CHUNK = 64
GLA_SUB = 16
GLA_NSUB = GLA_CHUNK // GLA_SUB
GLA_EXP_CLAMP = 80.0

MIX_WIDTH = ATT_WIDTH + POOL_WIDTH + GLA_WIDTH
IN_COLS = 4 * ATT_WIDTH - ATT_WIDTH + POOL_WIDTH + 4 * GLA_WIDTH + 2 * GLA_GATE_RANK
GATE_COL0 = IN_COLS - 2 * GLA_GATE_RANK
IN_COLS_PAD = GATE_COL0 + 128

V7X_LANES = 128
V7X_MXU_DIM = 256
SEQ_ALIGN = 128
SHORT_SEQ_ROWS = 2304
ATT_TILE_CAP = 1280
ATT_KV_TILE_LONG = 1280
ATT_QUERY_BLOCK = V7X_MXU_DIM
ATT_SKEW = 4
ROW_TILE_CAP = 704
GLA_TILE_CAP = 1088
FFN_TILE_CAP = 704
VMEM_LIMIT_BYTES = 56 * 1024 * 1024
FF_TILE = 256
LOG2E = 1.4426950408889634


def _divisor_tile(n, cap, mult):
    best = None
    for t in range(mult, min(n, cap) + 1, mult):
        if n % t == 0:
            best = t
    assert best is not None, (n, cap, mult)
    return best


def _plan(seq_len):
    if seq_len <= SHORT_SEQ_ROWS:
        lp = -(-seq_len // SEQ_ALIGN) * SEQ_ALIGN
        tk = tq = lp
    else:
        tk = ATT_KV_TILE_LONG
        lp = -(-seq_len // tk) * tk
        tq = _divisor_tile(lp, ATT_TILE_CAP, V7X_LANES)
    return dict(
        lp=lp,
        row_tile=_divisor_tile(lp, ROW_TILE_CAP, 16),
        tq=tq,
        tk=tk,
        gla_tile=_divisor_tile(lp, GLA_TILE_CAP, GLA_CHUNK),
    )


def _params(sem):
    return pltpu.CompilerParams(dimension_semantics=sem, vmem_limit_bytes=VMEM_LIMIT_BYTES)


def _dot(a, b):
    return jnp.dot(a, b, preferred_element_type=F32)


def _dot_nt(a, b):
    return lax.dot_general(a, b, (((1,), (1,)), ((), ())), preferred_element_type=F32)


def _dot_tn(a, b):
    return lax.dot_general(a, b, (((0,), (0,)), ((), ())), preferred_element_type=F32)


def _split2(x):
    hi = x.astype(BF16)
    lo = (x - hi.astype(F32)).astype(BF16)
    return hi, lo


def _group_mean_sq(x, bd):
    hi, lo = _split2(x * x)
    outs = []
    for c in range(x.shape[1] // V7X_MXU_DIM):
        sl = slice(V7X_MXU_DIM * c, V7X_MXU_DIM * (c + 1))
        outs.append(_dot(hi[:, sl], bd) + _dot(lo[:, sl], bd))
    return outs[0] if len(outs) == 1 else jnp.concatenate(outs, axis=1)


def _ffn_kernel(x_ref, g_ref, wi_ref, wo_ref, o_ref):
    x = x_ref[...]
    ms = jnp.mean(x * x, axis=-1, keepdims=True)
    xn = (x * lax.rsqrt(ms + RMS_EPS) * g_ref[...]).astype(BF16)
    acc = None
    for c0 in range(0, D_FF, FF_TILE):
        g = _dot(xn, wi_ref[:, c0:c0 + FF_TILE])
        u = _dot(xn, wi_ref[:, D_FF + c0:D_FF + c0 + FF_TILE])
        a = (g * jax.nn.sigmoid(g) * u).astype(BF16)
        d = _dot(a, wo_ref[c0:c0 + FF_TILE, :])
        acc = d if acc is None else acc + d
    o_ref[...] = x + 0.5 * acc


def _ffn(h2d, layer, norm, w_in, w_out):
    rows = h2d.shape[0]
    tm = _divisor_tile(rows, FFN_TILE_CAP, 16)
    resident = lambda *shape: pl.BlockSpec((None,) + shape, lambda i: (layer,) + (0,) * len(shape),
                                           pipeline_mode=pl.Buffered(1))
    return pl.pallas_call(
        _ffn_kernel,
        grid=(rows // tm,),
        in_specs=[
            pl.BlockSpec((tm, D_MODEL), lambda i: (i, 0)),
            pl.BlockSpec((None, 1, D_MODEL), lambda i: (layer, 0, 0)),
            resident(D_MODEL, 2 * D_FF),
            resident(D_FF, D_MODEL),
        ],
        out_specs=pl.BlockSpec((tm, D_MODEL), lambda i: (i, 0)),
        out_shape=jax.ShapeDtypeStruct((rows, D_MODEL), F32),
        compiler_params=_params(("parallel",)),
        name="ffn",
    )(h2d, norm, w_in, w_out)


def _rope128(x, cos, sin_signed, first_half):
    partner = jnp.where(first_half, pltpu.roll(x, 96, 1), pltpu.roll(x, 32, 1))
    return x * cos + partner * sin_signed


def _proj_kernel(seq_len, tm, h_ref, g_ref, w_ref, cos_ref, sin_ref, qg_ref, kg_ref, bd_ref, gu_ref, gbias_ref,
                 q_ref, k_ref, v_ref, pu_ref, gq_ref, gk_ref, gv_ref, gg_ref, gf_ref, gb_ref):
    i = pl.program_id(1)
    x = h_ref[...]
    pos = i * tm + lax.broadcasted_iota(jnp.int32, (tm, 1), 0)
    valid = pos < seq_len
    ms = jnp.mean(x * x, axis=-1, keepdims=True)
    xn = jnp.where(valid, x * lax.rsqrt(ms + RMS_EPS) * g_ref[...], 0.0).astype(BF16)

    bd = bd_ref[...]
    cos = cos_ref[...]
    sin = sin_ref[...]
    lane = lax.broadcasted_iota(jnp.int32, (tm, V7X_LANES), 1)
    first_half = (lane % ATT_HEAD_DIM) < (ATT_HEAD_DIM // 2)

    def qk_path(col0, gain_ref, out_ref, out_scale):
        z = _dot(xn, w_ref[:, col0:col0 + ATT_WIDTH])
        zn = z * lax.rsqrt(_group_mean_sq(z, bd) + RMS_EPS)
        gain = gain_ref[...]
        for c in range(ATT_WIDTH // V7X_LANES):
            sl = slice(V7X_LANES * c, V7X_LANES * (c + 1))
            r = _rope128(zn[:, sl] * gain, cos, sin, first_half)
            out_ref[:, sl] = (r * out_scale).astype(BF16)

    qk_path(0, qg_ref, q_ref, ATT_HEAD_DIM ** -0.5 * LOG2E)
    qk_path(ATT_WIDTH, kg_ref, k_ref, 1.0)

    v_ref[...] = _dot(xn, w_ref[:, 2 * ATT_WIDTH:3 * ATT_WIDTH]).astype(BF16)

    c0 = 3 * ATT_WIDTH
    pu_ref[...] = _dot(xn, w_ref[:, c0:c0 + POOL_WIDTH])
    c0 += POOL_WIDTH
    gq_ref[...] = (_dot(xn, w_ref[:, c0:c0 + GLA_WIDTH]) * (GLA_HEAD_DIM ** -0.5)).astype(BF16)
    c0 += GLA_WIDTH
    gk_ref[...] = _dot(xn, w_ref[:, c0:c0 + GLA_WIDTH]).astype(BF16)
    c0 += GLA_WIDTH
    gv_ref[...] = _dot(xn, w_ref[:, c0:c0 + GLA_WIDTH]).astype(BF16)
    c0 += GLA_WIDTH
    gg_ref[...] = _dot(xn, w_ref[:, c0:c0 + GLA_WIDTH])

    glr = _dot(xn, w_ref[:, GATE_COL0:IN_COLS_PAD])
    glr_hi, glr_lo = _split2(glr)
    gu_hi, gu_lo = _split2(gu_ref[...])
    logit = _dot(glr_hi, gu_hi) + _dot(glr_lo, gu_hi) + _dot(glr_hi, gu_lo) + gbias_ref[...]
    log_sig = jnp.minimum(logit, 0.0) - jnp.log(1.0 + jnp.exp(-jnp.abs(logit)))
    log_a = log_sig * (1.0 / GLA_GATE_TEMP)
    gf_ref[...] = log_a[:, :GLA_WIDTH]
    gb_ref[...] = log_a[:, GLA_WIDTH:]


def _proj(h, layer, seq_len, plan, prep):
    b, lp, _ = h.shape
    tm = plan["row_tile"]
    row = lambda w: pl.BlockSpec((None, tm, w), lambda bb, i: (bb, i, 0))
    par = lambda *shape: pl.BlockSpec((None,) + shape, lambda bb, i: (layer,) + (0,) * len(shape))
    out_widths = [(ATT_WIDTH, BF16), (ATT_WIDTH, BF16), (ATT_WIDTH, BF16), (POOL_WIDTH, F32),
                  (GLA_WIDTH, BF16), (GLA_WIDTH, BF16), (GLA_WIDTH, BF16), (GLA_WIDTH, F32),
                  (GLA_WIDTH, F32), (GLA_WIDTH, F32)]
    return pl.pallas_call(
        functools.partial(_proj_kernel, seq_len, tm),
        grid=(b, lp // tm),
        in_specs=[
            row(D_MODEL),
            par(1, D_MODEL),
            par(D_MODEL, IN_COLS_PAD),
            pl.BlockSpec((tm, V7X_LANES), lambda bb, i: (i, 0)),
            pl.BlockSpec((tm, V7X_LANES), lambda bb, i: (i, 0)),
            par(1, V7X_LANES),
            par(1, V7X_LANES),
            pl.BlockSpec((V7X_MXU_DIM, V7X_MXU_DIM), lambda bb, i: (0, 0)),
            par(V7X_LANES, 2 * GLA_WIDTH),
            par(1, 2 * GLA_WIDTH),
        ],
        out_specs=[row(w) for w, _ in out_widths],
        out_shape=[jax.ShapeDtypeStruct((b, lp, w), dt) for w, dt in out_widths],
        compiler_params=_params(("parallel", "parallel")),
        name="proj",
    )(h, prep["mix_norm"], prep["w_in"], prep["cos"][lp], prep["sin"][lp], prep["q_gain"], prep["k_gain"],
      prep["bd"], prep["gate_up"], prep["gate_bias"])


def _attn_kernel(seq_len, tq, tk, qb, nk, lam_init, q_ref, k_ref, v_ref, lam_ref, sn_ref, o_ref,
                 m_ref, acc_ref):
    ik = pl.program_id(3)

    @pl.when(ik == 0)
    def _():
        m_ref[...] = jnp.full_like(m_ref, -jnp.inf)
        acc_ref[...] = jnp.zeros_like(acc_ref)

    def step(mask_tail):
        k = k_ref[...]
        ones_rows = jnp.where(lax.broadcasted_iota(jnp.int32, (ATT_ACC_ROWS - ATT_V_DIM, tk), 0) == 0, 1.0, 0.0)
        vt = jnp.concatenate([v_ref[...].T, ones_rows.astype(BF16)], axis=0)
        lane = lax.broadcasted_iota(jnp.int32, k.shape, 1)
        halves = (jnp.where(lane < ATT_HEAD_DIM, k, jnp.zeros_like(k)),
                  jnp.where(lane >= ATT_HEAD_DIM, k, jnp.zeros_like(k)))
        if mask_tail:
            key_ok = (ik * tk + lax.broadcasted_iota(jnp.int32, (tk, 1), 0)) < seq_len
        chains = [(slice(c0, min(c0 + qb, tq)), c) for c0 in range(0, tq, qb) for c in (0, 1)]

        def scores(cols, c):
            return _dot_nt(halves[c], q_ref[cols, :])

        def softmax_update(cols, c, st):
            if mask_tail:
                st = jnp.where(key_ok, st, -jnp.inf)
            m_prev = m_ref[c, :, cols]
            m_new = jnp.maximum(m_prev, jnp.max(st, axis=0, keepdims=True))
            alpha = jnp.exp2(m_prev - m_new)
            pt = jnp.exp2(st - m_new)
            acc_ref[c, :, cols] = alpha * acc_ref[c, :, cols] + _dot(vt, pt.astype(BF16))
            m_ref[c, :, cols] = m_new

        pending = {}
        for i in range(len(chains) + ATT_SKEW):
            if i < len(chains):
                pending[i] = scores(*chains[i])
            if i >= ATT_SKEW:
                softmax_update(*chains[i - ATT_SKEW], pending.pop(i - ATT_SKEW))

    if nk * tk == seq_len:
        step(False)
    elif nk == 1:
        step(True)
    else:
        pl.when(ik < nk - 1)(lambda: step(False))
        pl.when(ik == nk - 1)(lambda: step(True))

    @pl.when(ik == nk - 1)
    def _():
        lp = lam_ref[...]
        a1 = jnp.sum(lp[0:1] * lp[1:2], axis=-1, keepdims=True)
        a2 = jnp.sum(lp[2:3] * lp[3:4], axis=-1, keepdims=True)
        lam = jnp.exp(a1) - jnp.exp(a2) + lam_init
        num = acc_ref[:, :ATT_V_DIM, :]
        den = acc_ref[:, ATT_V_DIM:ATT_V_DIM + 1, :]
        ot = num[0] / den[0] - lam * (num[1] / den[1])
        ms = jnp.mean(ot * ot, axis=0, keepdims=True)
        ot = ot * (lax.rsqrt(ms + RMS_EPS) * (1.0 - lam_init))
        o_ref[...] = (ot.T * sn_ref[...]).astype(BF16)


def _attn(q, k, v, layer, seq_len, plan, prep, lam_init):
    b, lp, _ = q.shape
    tq, tk = plan["tq"], plan["tk"]
    nk = lp // tk
    return pl.pallas_call(
        functools.partial(_attn_kernel, seq_len, tq, tk, ATT_QUERY_BLOCK, nk, lam_init),
        grid=(b, ATT_HEADS, lp // tq, nk),
        in_specs=[
            pl.BlockSpec((None, tq, ATT_V_DIM), lambda bb, hh, iq, ik: (bb, iq, hh)),
            pl.BlockSpec((None, tk, ATT_V_DIM), lambda bb, hh, iq, ik: (bb, ik, hh)),
            pl.BlockSpec((None, tk, ATT_V_DIM), lambda bb, hh, iq, ik: (bb, ik, hh)),
            pl.BlockSpec((None, 4, ATT_HEAD_DIM), lambda bb, hh, iq, ik: (layer, 0, 0)),
            pl.BlockSpec((None, 1, ATT_V_DIM), lambda bb, hh, iq, ik: (layer, 0, 0)),
        ],
        out_specs=pl.BlockSpec((None, tq, ATT_V_DIM), lambda bb, hh, iq, ik: (bb, iq, hh)),
        out_shape=jax.ShapeDtypeStruct((b, lp, ATT_WIDTH), BF16),
        scratch_shapes=[pltpu.VMEM((2, 1, tq), F32), pltpu.VMEM((2, ATT_ACC_ROWS, tq), F32)],
        compiler_params=_params(("parallel", "parallel", "parallel", "arbitrary")),
        name="attn",
    )(q, k, v, prep["lam"], prep["sub_norm"])


def _gla_kernel(reverse, nchunk, q_ref, k_ref, v_ref, g_ref, o_ref, st_ref):
    c_rows = GLA_CHUNK

    @pl.when(pl.program_id(1) == 0)
    def _():
        st_ref[...] = jnp.zeros_like(st_ref)

    def before(a, b_):
        return (a >= b_) if reverse else (a <= b_)

    ti = lax.broadcasted_iota(jnp.int32, (2 * c_rows, c_rows), 0)
    si = lax.broadcasted_iota(jnp.int32, (2 * c_rows, c_rows), 1)
    t_loc = ti % c_rows
    incl = before(si, t_loc)
    blk = jnp.logical_and(before(si // GLA_SUB, t_loc // GLA_SUB), (si // GLA_SUB) != (t_loc // GLA_SUB))
    cum_mat = jnp.where(ti < c_rows, jnp.where(incl, 1.0, 0.0), jnp.where(blk, 1.0, 0.0)).astype(BF16)

    hr = lax.broadcasted_iota(jnp.int32, (GLA_WIDTH, GLA_WIDTH), 0) // GLA_HEAD_DIM
    hc = lax.broadcasted_iota(jnp.int32, (GLA_WIDTH, GLA_WIDTH), 1) // GLA_HEAD_DIM
    head_mask = hr == hc
    tt = lax.broadcasted_iota(jnp.int32, (c_rows, GLA_WIDTH), 0)
    ss = lax.broadcasted_iota(jnp.int32, (c_rows, GLA_WIDTH), 1) % c_rows
    causal = before(ss, tt)
    row_blk = lax.broadcasted_iota(jnp.int32, (c_rows, 1), 0) // GLA_SUB

    def rows_of(ci):
        c = (nchunk - 1 - ci) if reverse else ci
        return slice(c * c_rows, (c + 1) * c_rows)

    def stage_cumsum(ci):
        g_hi, g_lo = _split2(g_ref[rows_of(ci), :])
        return _dot(cum_mat, g_hi) + _dot(cum_mat, g_lo)

    def stage_intra(ci, cums):
        rows = rows_of(ci)
        q = q_ref[rows, :].astype(F32)
        k = k_ref[rows, :].astype(F32)
        b_incl = cums[:c_rows]
        b_ref = cums[c_rows:]
        b_last = b_incl[0:1] if reverse else b_incl[c_rows - 1:c_rows]
        qs = q * jnp.exp(b_incl - b_ref)
        qe = (q * jnp.exp(b_incl)).astype(BF16)
        kd = (k * jnp.exp(b_last - b_incl)).astype(BF16)
        lhs_parts, rhs_parts = [], []
        for i in range(GLA_NSUB):
            r_i = b_ref[GLA_SUB * i:GLA_SUB * i + 1]
            k_i = jnp.where(before(row_blk, i), k * jnp.exp(jnp.minimum(r_i - b_incl, GLA_EXP_CLAMP)), 0.0)
            k_i4 = jnp.concatenate([k_i] * GLA_HEADS, axis=0)
            rhs_parts.append(jnp.where(head_mask, k_i4, 0.0).astype(BF16))
            lhs_parts.append(jnp.where(row_blk == i, qs, 0.0).astype(BF16))
        lhs = jnp.concatenate(lhs_parts, axis=1)
        rhs = jnp.concatenate(rhs_parts, axis=1)
        a = jnp.where(causal, _dot_nt(lhs, rhs), 0.0).astype(BF16)
        return a, qe, kd, jnp.exp(b_last)

    def stage_state(ci, st, a, qe, kd, decay):
        rows = rows_of(ci)
        v = v_ref[rows, :]
        v4 = jnp.where(head_mask, jnp.concatenate([v] * GLA_HEADS, axis=0), jnp.zeros((), BF16))
        o_ref[rows, :] = _dot(a, v4) + _dot_nt(qe, st.astype(BF16))
        return st * decay + jnp.where(head_mask, _dot_tn(v, kd), 0.0)

    st = st_ref[...]
    cums, intra = {}, {}
    for step in range(nchunk + 2):
        if step < nchunk:
            cums[step] = stage_cumsum(step)
        if 0 <= step - 1 < nchunk:
            intra[step - 1] = stage_intra(step - 1, cums.pop(step - 1))
        if 0 <= step - 2 < nchunk:
            st = stage_state(step - 2, st, *intra.pop(step - 2))
    st_ref[...] = st


def _gla(gq, gk, gv, gdecay, plan, reverse):
    b, lp, _ = gq.shape
    tt = plan["gla_tile"]
    nt = lp // tt
    idx = (lambda bb, i: (bb, nt - 1 - i, 0)) if reverse else (lambda bb, i: (bb, i, 0))
    spec = pl.BlockSpec((None, tt, GLA_WIDTH), idx)
    return pl.pallas_call(
        functools.partial(_gla_kernel, reverse, tt // GLA_CHUNK),
        grid=(b, nt),
        in_specs=[spec, spec, spec, spec],
        out_specs=spec,
        out_shape=jax.ShapeDtypeStruct((b, lp, GLA_WIDTH), F32),
        scratch_shapes=[pltpu.VMEM((GLA_WIDTH, GLA_WIDTH), F32)],
        compiler_params=_params(("parallel", "arbitrary")),
        name="gla_bwd" if reverse else "gla_fwd",
    )(gq, gk, gv, gdecay)


def _post_kernel(seq_len, tm, nt, h_ref, oa_ref, pu_ref, pup_ref, pun_ref, of_ref, ob_ref, gg_ref,
                 pw_ref, ps_ref, gn_ref, bd_ref, wo_ref, o_ref):
    i = pl.program_id(1)
    cur = pu_ref[...]
    prev = jnp.where(i > 0, pup_ref[...], 0.0)
    nxt = jnp.where(i < nt - 1, pun_ref[...], 0.0)
    ext = jnp.concatenate([prev, cur, nxt], axis=0)
    n = tm + 2 * POOL_HALO
    p2 = ext + pltpu.roll(ext, 1, 0)
    p4 = p2 + pltpu.roll(p2, 2, 0)
    p8 = p4 + pltpu.roll(p4, 4, 0)
    p16 = p8 + pltpu.roll(p8, 8, 0)
    sums = (p2, pltpu.roll(p4, n - 1, 0), pltpu.roll(p8, n - 3, 0), pltpu.roll(p16, n - 7, 0))

    pos = i * tm + lax.broadcasted_iota(jnp.int32, (tm, 1), 0)
    grp = lax.broadcasted_iota(jnp.int32, (tm, POOL_WIDTH), 1) // POOL_GROUP_DIM
    wsum = jnp.zeros((tm, POOL_WIDTH), F32)
    cnt = jnp.ones((tm, POOL_WIDTH), F32)
    for gi, win in enumerate(POOL_WINDOWS):
        lo = win // 2
        hi = win - lo - 1
        c = jnp.minimum(pos + hi + 1, seq_len) - jnp.maximum(pos - lo, 0)
        c = jnp.maximum(c, 1).astype(F32)
        sel = grp == gi
        wsum = jnp.where(sel, sums[gi][POOL_HALO:POOL_HALO + tm], wsum)
        cnt = jnp.where(sel, c, cnt)
    pooled = (wsum / cnt - cur).astype(BF16)
    y_pool = (_dot(pooled, pw_ref[...]) * ps_ref[...]).astype(BF16)

    og = of_ref[...] + ob_ref[...]
    og = og * lax.rsqrt(_group_mean_sq(og, bd_ref[...]) + RMS_EPS) * gn_ref[...]
    gate = gg_ref[...]
    y_gla = (og * (gate * jax.nn.sigmoid(gate))).astype(BF16)

    mixed = (_dot(oa_ref[...], wo_ref[0:ATT_WIDTH, :])
             + _dot(y_pool, wo_ref[ATT_WIDTH:ATT_WIDTH + POOL_WIDTH, :])
             + _dot(y_gla, wo_ref[ATT_WIDTH + POOL_WIDTH:MIX_WIDTH, :]))
    o_ref[...] = h_ref[...] + mixed


def _post(h, o_att, pu, o_f, o_b, gg, layer, seq_len, plan, prep):
    b, lp, _ = h.shape
    tm = plan["row_tile"]
    nt = lp // tm
    hb = tm // POOL_HALO
    row = lambda w: pl.BlockSpec((None, tm, w), lambda bb, i: (bb, i, 0))
    par = lambda *shape: pl.BlockSpec((None,) + shape, lambda bb, i: (layer,) + (0,) * len(shape))
    return pl.pallas_call(
        functools.partial(_post_kernel, seq_len, tm, nt),
        grid=(b, nt),
        in_specs=[
            row(D_MODEL),
            row(ATT_WIDTH),
            row(POOL_WIDTH),
            pl.BlockSpec((None, POOL_HALO, POOL_WIDTH), lambda bb, i: (bb, jnp.maximum(i * hb - 1, 0), 0)),
            pl.BlockSpec((None, POOL_HALO, POOL_WIDTH),
                         lambda bb, i: (bb, jnp.minimum((i + 1) * hb, lp // POOL_HALO - 1), 0)),
            row(GLA_WIDTH),
            row(GLA_WIDTH),
            row(GLA_WIDTH),
            par(POOL_WIDTH, POOL_WIDTH),
            par(1, POOL_WIDTH),
            par(1, GLA_WIDTH),
            pl.BlockSpec((V7X_MXU_DIM, V7X_MXU_DIM), lambda bb, i: (0, 0)),
            par(MIX_WIDTH, D_MODEL),
        ],
        out_specs=row(D_MODEL),
        out_shape=jax.ShapeDtypeStruct((b, lp, D_MODEL), F32),
        compiler_params=_params(("parallel", "parallel")),
        name="post",
    )(h, o_att, pu, pu, pu, o_f, o_b, gg, prep["pool_w"], prep["pool_scale"], prep["gla_norm"], prep["bd"],
      prep["w_out"])


def _rope_tables(lp):
    pos = jnp.arange(lp, dtype=F32)
    inv_freq = ROPE_THETA ** (-jnp.arange(0, ATT_HEAD_DIM, 2, dtype=F32) / ATT_HEAD_DIM)
    ang = pos[:, None] * inv_freq[None, :]
    cos, sin = jnp.cos(ang), jnp.sin(ang)
    return jnp.tile(cos, (1, 4)), jnp.tile(jnp.concatenate([-sin, sin], axis=1), (1, 2))


def _prepare(p, padded_lengths):
    eye = jnp.eye(V7X_MXU_DIM // ATT_HEAD_DIM, dtype=F32)
    bd = jnp.kron(eye, jnp.full((ATT_HEAD_DIM, ATT_HEAD_DIM), 1.0 / ATT_HEAD_DIM, F32)).astype(BF16)
    w_in = jnp.pad(p["w_in"], ((0, 0), (0, 0), (0, IN_COLS_PAD - IN_COLS))).astype(BF16)
    gu = jnp.zeros((DEPTH, V7X_LANES, 2 * GLA_WIDTH), F32)
    gu = gu.at[:, :GLA_GATE_RANK, :GLA_WIDTH].set(p["gla_gate_up"][:, 0])
    gu = gu.at[:, GLA_GATE_RANK:2 * GLA_GATE_RANK, GLA_WIDTH:].set(p["gla_gate_up"][:, 1])
    pool_w = jnp.zeros((DEPTH, POOL_WIDTH, POOL_WIDTH), F32)
    for gi in range(len(POOL_WINDOWS)):
        sl = slice(gi * POOL_GROUP_DIM, (gi + 1) * POOL_GROUP_DIM)
        pool_w = pool_w.at[:, sl, sl].set(p["pool_w"][:, gi])
    tables = {lp: _rope_tables(lp) for lp in padded_lengths}
    return dict(
        bd=bd,
        w_in=w_in,
        w_out=p["w_out"].astype(BF16),
        mix_norm=p["mix_norm"][:, None, :],
        q_gain=jnp.tile(p["attn_q_norm"], (1, 2))[:, None, :],
        k_gain=jnp.tile(p["attn_k_norm"], (1, 2))[:, None, :],
        lam=jnp.stack([p["lambda_q1"], p["lambda_k1"], p["lambda_q2"], p["lambda_k2"]], axis=1),
        sub_norm=p["attn_sub_norm"][:, None, :],
        gate_up=gu,
        gate_bias=p["gla_gate_bias"].reshape(DEPTH, 1, 2 * GLA_WIDTH),
        pool_w=pool_w.astype(BF16),
        pool_scale=p["pool_scale"][:, None, :],
        gla_norm=jnp.tile(p["gla_out_norm"], (1, GLA_HEADS))[:, None, :],
        ffn1_norm=p["ffn1_norm"][:, None, :],
        ffn2_norm=p["ffn2_norm"][:, None, :],
        ffn1_w_in=p["ffn1_w_in"].astype(BF16),
        ffn1_w_out=p["ffn1_w_out"].astype(BF16),
        ffn2_w_in=p["ffn2_w_in"].astype(BF16),
        ffn2_w_out=p["ffn2_w_out"].astype(BF16),
        cos={lp: t[0] for lp, t in tables.items()},
        sin={lp: t[1] for lp, t in tables.items()},
    )


def _trunk(x, meta_tokens, prep):
    b, s, _ = x.shape
    seq_len = s + N_META
    plan = _plan(seq_len)
    lp = plan["lp"]
    meta = jnp.broadcast_to(meta_tokens.astype(x.dtype)[None], (b, N_META, D_MODEL))
    h = jnp.concatenate([meta, x, jnp.zeros((b, lp - seq_len, D_MODEL), x.dtype)], axis=1)
    for layer in range(DEPTH):
        lam_init = 0.8 - 0.6 * math.exp(-0.3 * layer)
        h = _ffn(h.reshape(b * lp, D_MODEL), layer, prep["ffn1_norm"], prep["ffn1_w_in"],
                 prep["ffn1_w_out"]).reshape(b, lp, D_MODEL)
        q, k, v, pu, gq, gk, gv, gg, gf, gb = _proj(h, layer, seq_len, plan, prep)
        o_att = _attn(q, k, v, layer, seq_len, plan, prep, lam_init)
        o_f = _gla(gq, gk, gv, gf, plan, reverse=False)
        o_b = _gla(gq, gk, gv, gb, plan, reverse=True)
        h = _post(h, o_att, pu, o_f, o_b, gg, layer, seq_len, plan, prep)
        h = _ffn(h.reshape(b * lp, D_MODEL), layer, prep["ffn2_norm"], prep["ffn2_w_in"],
                 prep["ffn2_w_out"]).reshape(b, lp, D_MODEL)
    return h[:, N_META:seq_len]


def kernel(x_prompt, x_sample, meta_tokens, ffn1_norm, ffn1_w_in, ffn1_w_out, mix_norm, w_in, w_out, attn_q_norm,
           attn_k_norm, lambda_q1, lambda_k1, lambda_q2, lambda_k2, attn_sub_norm, pool_w, pool_scale, gla_gate_up,
           gla_gate_bias, gla_out_norm, ffn2_norm, ffn2_w_in, ffn2_w_out):
    p = dict(ffn1_norm=ffn1_norm, ffn1_w_in=ffn1_w_in, ffn1_w_out=ffn1_w_out, mix_norm=mix_norm, w_in=w_in,
             w_out=w_out, attn_q_norm=attn_q_norm, attn_k_norm=attn_k_norm, lambda_q1=lambda_q1,
             lambda_k1=lambda_k1, lambda_q2=lambda_q2, lambda_k2=lambda_k2, attn_sub_norm=attn_sub_norm,
             pool_w=pool_w, pool_scale=pool_scale, gla_gate_up=gla_gate_up, gla_gate_bias=gla_gate_bias,
             gla_out_norm=gla_out_norm, ffn2_norm=ffn2_norm, ffn2_w_in=ffn2_w_in, ffn2_w_out=ffn2_w_out)
    lengths = {_plan(x.shape[1] + N_META)["lp"] for x in (x_prompt, x_sample)}
    prep = _prepare(p, lengths)
    return (_trunk(x_prompt, meta_tokens, prep), _trunk(x_sample, meta_tokens, prep))
```

```python
import functools
import math

import jax
import jax.numpy as jnp
from jax import lax
from jax.experimental import pallas as pl
from jax.experimental.pallas import tpu as pltpu

F32 = jnp.float32
BF16 = jnp.bfloat16

D_MODEL = 1024
D_FF = 2816
DEPTH = 4
N_META = 16
RMS_EPS = 1e-6
ROPE_THETA = 10000.0

ATT_HEADS = 4
ATT_HEAD_DIM = 64
ATT_V_DIM = 2 * ATT_HEAD_DIM
ATT_WIDTH = ATT_HEADS * ATT_V_DIM
ATT_ACC_ROWS = ATT_V_DIM + 16

POOL_WINDOWS = (2, 4, 8, 16)
POOL_GROUP_DIM = 64
POOL_WIDTH = len(POOL_WINDOWS) * POOL_GROUP_DIM
POOL_HALO = 8

GLA_HEADS = 4
GLA_HEAD_DIM = 64
GLA_WIDTH = GLA_HEADS * GLA_HEAD_DIM
GLA_GATE_RANK = 16
GLA_GATE_TEMP = 16.0
GLA_CHUNK = 64
GLA_SUB = 16
GLA_NSUB = GLA_CHUNK // GLA_SUB
GLA_EXP_CLAMP = 80.0

MIX_WIDTH = ATT_WIDTH + POOL_WIDTH + GLA_WIDTH
IN_COLS = 4 * ATT_WIDTH - ATT_WIDTH + POOL_WIDTH + 4 * GLA_WIDTH + 2 * GLA_GATE_RANK
GATE_COL0 = IN_COLS - 2 * GLA_GATE_RANK
IN_COLS_PAD = GATE_COL0 + 128

V7X_LANES = 128
V7X_MXU_DIM = 256
SEQ_ALIGN = 128
SHORT_SEQ_ROWS = 2304
ATT_Q_ROWS_CAP = 16640
ATT_KV_TILE_LONG = 1280
ATT_QUERY_BLOCK = V7X_MXU_DIM
ATT_KEY_BLOCK = V7X_MXU_DIM
ATT_SKEW = 5
ROW_TILE_CAP = 704
GLA_TILE_CAP = 1088
FFN_TILE_CAP = 704
VMEM_LIMIT_BYTES = 56 * 1024 * 1024
FF_TILE = 256
LOG2E = 1.4426950408889634


def _divisor_tile(n, cap, mult):
    best = None
    for t in range(mult, min(n, cap) + 1, mult):
        if n % t == 0:
            best = t
    assert best is not None, (n, cap, mult)
    return best


def _plan(seq_len):
    if seq_len <= SHORT_SEQ_ROWS:
        lp = -(-seq_len // SEQ_ALIGN) * SEQ_ALIGN
        tk = tq = sub = lp
    else:
        tk = sub = ATT_KV_TILE_LONG
        lp = -(-seq_len // tk) * tk
        tq = _divisor_tile(lp, ATT_Q_ROWS_CAP, sub)
    return dict(
        lp=lp,
        row_tile=_divisor_tile(lp, ROW_TILE_CAP, 16),
        tq=tq,
        tk=tk,
        sub=sub,
        gla_tile=_divisor_tile(lp, GLA_TILE_CAP, GLA_CHUNK),
    )


def _params(sem):
    return pltpu.CompilerParams(dimension_semantics=sem, vmem_limit_bytes=VMEM_LIMIT_BYTES)


def _dot(a, b):
    return jnp.dot(a, b, preferred_element_type=F32)


def _dot_nt(a, b):
    return lax.dot_general(a, b, (((1,), (1,)), ((), ())), preferred_element_type=F32)


def _dot_tn(a, b):
    return lax.dot_general(a, b, (((0,), (0,)), ((), ())), preferred_element_type=F32)


def _split2(x):
    hi = x.astype(BF16)
    lo = (x - hi.astype(F32)).astype(BF16)
    return hi, lo


def _group_mean_sq(x, bd):
    hi, lo = _split2(x * x)
    outs = []
    for c in range(x.shape[1] // V7X_MXU_DIM):
        sl = slice(V7X_MXU_DIM * c, V7X_MXU_DIM * (c + 1))
        outs.append(_dot(hi[:, sl], bd) + _dot(lo[:, sl], bd))
    return outs[0] if len(outs) == 1 else jnp.concatenate(outs, axis=1)


def _ffn_kernel(x_ref, g_ref, wi_ref, wo_ref, o_ref):
    x = x_ref[...]
    ms = jnp.mean(x * x, axis=-1, keepdims=True)
    xn = (x * lax.rsqrt(ms + RMS_EPS) * g_ref[...]).astype(BF16)
    acc = None
    for c0 in range(0, D_FF, FF_TILE):
        g = _dot(xn, wi_ref[:, c0:c0 + FF_TILE])
        u = _dot(xn, wi_ref[:, D_FF + c0:D_FF + c0 + FF_TILE])
        a = (g * jax.nn.sigmoid(g) * u).astype(BF16)
        d = _dot(a, wo_ref[c0:c0 + FF_TILE, :])
        acc = d if acc is None else acc + d
    o_ref[...] = x + 0.5 * acc


def _ffn(h2d, layer, norm, w_in, w_out):
    rows = h2d.shape[0]
    tm = _divisor_tile(rows, FFN_TILE_CAP, 16)
    resident = lambda *shape: pl.BlockSpec((None,) + shape, lambda i: (layer,) + (0,) * len(shape),
                                           pipeline_mode=pl.Buffered(1))
    return pl.pallas_call(
        _ffn_kernel,
        grid=(rows // tm,),
        in_specs=[
            pl.BlockSpec((tm, D_MODEL), lambda i: (i, 0)),
            pl.BlockSpec((None, 1, D_MODEL), lambda i: (layer, 0, 0)),
            resident(D_MODEL, 2 * D_FF),
            resident(D_FF, D_MODEL),
        ],
        out_specs=pl.BlockSpec((tm, D_MODEL), lambda i: (i, 0)),
        out_shape=jax.ShapeDtypeStruct((rows, D_MODEL), F32),
        compiler_params=_params(("parallel",)),
        name="ffn",
    )(h2d, norm, w_in, w_out)


def _rope128(x, cos, sin_signed, first_half):
    partner = jnp.where(first_half, pltpu.roll(x, 96, 1), pltpu.roll(x, 32, 1))
    return x * cos + partner * sin_signed


def _proj_kernel(seq_len, tm, h_ref, g_ref, w_ref, cos_ref, sin_ref, qg_ref, kg_ref, bd_ref, gu_ref, gbias_ref,
                 q_ref, k_ref, v_ref, pu_ref, gq_ref, gk_ref, gv_ref, gg_ref, gf_ref, gb_ref):
    i = pl.program_id(1)
    x = h_ref[...]
    pos = i * tm + lax.broadcasted_iota(jnp.int32, (tm, 1), 0)
    valid = pos < seq_len
    ms = jnp.mean(x * x, axis=-1, keepdims=True)
    xn = jnp.where(valid, x * lax.rsqrt(ms + RMS_EPS) * g_ref[...], 0.0).astype(BF16)

    bd = bd_ref[...]
    cos = cos_ref[...]
    sin = sin_ref[...]
    lane = lax.broadcasted_iota(jnp.int32, (tm, V7X_LANES), 1)
    first_half = (lane % ATT_HEAD_DIM) < (ATT_HEAD_DIM // 2)

    def qk_path(col0, gain_ref, out_ref, out_scale):
        z = _dot(xn, w_ref[:, col0:col0 + ATT_WIDTH])
        zn = z * lax.rsqrt(_group_mean_sq(z, bd) + RMS_EPS)
        gain = gain_ref[...]
        for c in range(ATT_WIDTH // V7X_LANES):
            sl = slice(V7X_LANES * c, V7X_LANES * (c + 1))
            r = _rope128(zn[:, sl] * gain, cos, sin, first_half)
            out_ref[:, sl] = (r * out_scale).astype(BF16)

    qk_path(0, qg_ref, q_ref, ATT_HEAD_DIM ** -0.5 * LOG2E)
    qk_path(ATT_WIDTH, kg_ref, k_ref, 1.0)

    v_ref[...] = _dot(xn, w_ref[:, 2 * ATT_WIDTH:3 * ATT_WIDTH]).astype(BF16)

    c0 = 3 * ATT_WIDTH
    pu_ref[...] = _dot(xn, w_ref[:, c0:c0 + POOL_WIDTH])
    c0 += POOL_WIDTH
    gq_ref[...] = (_dot(xn, w_ref[:, c0:c0 + GLA_WIDTH]) * (GLA_HEAD_DIM ** -0.5)).astype(BF16)
    c0 += GLA_WIDTH
    gk_ref[...] = _dot(xn, w_ref[:, c0:c0 + GLA_WIDTH]).astype(BF16)
    c0 += GLA_WIDTH
    gv_ref[...] = _dot(xn, w_ref[:, c0:c0 + GLA_WIDTH]).astype(BF16)
    c0 += GLA_WIDTH
    gg_ref[...] = _dot(xn, w_ref[:, c0:c0 + GLA_WIDTH])

    glr = _dot(xn, w_ref[:, GATE_COL0:IN_COLS_PAD])
    glr_hi, glr_lo = _split2(glr)
    gu_hi, gu_lo = _split2(gu_ref[...])
    logit = _dot(glr_hi, gu_hi) + _dot(glr_lo, gu_hi) + _dot(glr_hi, gu_lo) + gbias_ref[...]
    log_sig = jnp.minimum(logit, 0.0) - jnp.log(1.0 + jnp.exp(-jnp.abs(logit)))
    log_a = log_sig * (1.0 / GLA_GATE_TEMP)
    gf_ref[...] = log_a[:, :GLA_WIDTH]
    gb_ref[...] = log_a[:, GLA_WIDTH:]


def _proj(h, layer, seq_len, plan, prep):
    b, lp, _ = h.shape
    tm = plan["row_tile"]
    row = lambda w: pl.BlockSpec((None, tm, w), lambda bb, i: (bb, i, 0))
    par = lambda *shape: pl.BlockSpec((None,) + shape, lambda bb, i: (layer,) + (0,) * len(shape))
    out_widths = [(ATT_WIDTH, BF16), (ATT_WIDTH, BF16), (ATT_WIDTH, BF16), (POOL_WIDTH, F32),
                  (GLA_WIDTH, BF16), (GLA_WIDTH, BF16), (GLA_WIDTH, BF16), (GLA_WIDTH, F32),
                  (GLA_WIDTH, F32), (GLA_WIDTH, F32)]
    return pl.pallas_call(
        functools.partial(_proj_kernel, seq_len, tm),
        grid=(b, lp // tm),
        in_specs=[
            row(D_MODEL),
            par(1, D_MODEL),
            par(D_MODEL, IN_COLS_PAD),
            pl.BlockSpec((tm, V7X_LANES), lambda bb, i: (i, 0)),
            pl.BlockSpec((tm, V7X_LANES), lambda bb, i: (i, 0)),
            par(1, V7X_LANES),
            par(1, V7X_LANES),
            pl.BlockSpec((V7X_MXU_DIM, V7X_MXU_DIM), lambda bb, i: (0, 0)),
            par(V7X_LANES, 2 * GLA_WIDTH),
            par(1, 2 * GLA_WIDTH),
        ],
        out_specs=[row(w) for w, _ in out_widths],
        out_shape=[jax.ShapeDtypeStruct((b, lp, w), dt) for w, dt in out_widths],
        compiler_params=_params(("parallel", "parallel")),
        name="proj",
    )(h, prep["mix_norm"], prep["w_in"], prep["cos"][lp], prep["sin"][lp], prep["q_gain"], prep["k_gain"],
      prep["bd"], prep["gate_up"], prep["gate_bias"])


def _attn_kernel(seq_len, sub, nsub, tk, qb, nk, lam_init, q_ref, k_ref, v_ref, lam_ref, sn_ref, o_ref,
                 m_ref, acc_ref):
    ik = pl.program_id(3)

    def for_each_sub_tile(body):
        if nsub == 1:
            body(0)
        else:
            lax.fori_loop(0, nsub, lambda t, carry: (body(t), carry)[1], 0)

    @pl.when(ik == 0)
    def _():
        m_ref[...] = jnp.full_like(m_ref, -jnp.inf)
        acc_ref[...] = jnp.zeros_like(acc_ref)

    def step(mask_tail):
        k = k_ref[...]
        ones_rows = jnp.where(lax.broadcasted_iota(jnp.int32, (ATT_ACC_ROWS - ATT_V_DIM, tk), 0) == 0, 1.0, 0.0)
        vt = jnp.concatenate([v_ref[...].T, ones_rows.astype(BF16)], axis=0)
        lane = lax.broadcasted_iota(jnp.int32, k.shape, 1)
        halves = (jnp.where(lane < ATT_HEAD_DIM, k, jnp.zeros_like(k)),
                  jnp.where(lane >= ATT_HEAD_DIM, k, jnp.zeros_like(k)))
        if mask_tail:
            key_ok = (ik * tk + lax.broadcasted_iota(jnp.int32, (tk, 1), 0)) < seq_len
        kb = _divisor_tile(tk, ATT_KEY_BLOCK, V7X_MXU_DIM) if tk % V7X_MXU_DIM == 0 else tk
        chains = [(slice(r0, r0 + kb), c0, min(qb, sub - c0), c)
                  for r0 in range(0, tk, kb) for c0 in range(0, sub, qb) for c in (0, 1)]

        def sub_tile(t):
            base = t * sub if nsub == 1 else pl.multiple_of(t * sub, sub)

            def scores(keys, c0, width, c):
                return _dot_nt(halves[c][keys, :], q_ref[pl.ds(base + c0, width), :])

            def softmax_update(keys, c0, width, c, st):
                cols = slice(c0, c0 + width)
                if mask_tail:
                    st = jnp.where(key_ok[keys, :], st, -jnp.inf)
                m_prev = m_ref[t, c, :, cols]
                m_new = jnp.maximum(m_prev, jnp.max(st, axis=0, keepdims=True))
                alpha = jnp.exp2(m_prev - m_new)
                pt = jnp.exp2(st - m_new)
                acc_ref[t, c, :, cols] = alpha * acc_ref[t, c, :, cols] + _dot(vt[:, keys], pt.astype(BF16))
                m_ref[t, c, :, cols] = m_new

            pending = {}
            for i in range(len(chains) + ATT_SKEW):
                if i < len(chains):
                    pending[i] = scores(*chains[i])
                if i >= ATT_SKEW:
                    softmax_update(*chains[i - ATT_SKEW], pending.pop(i - ATT_SKEW))

        for_each_sub_tile(sub_tile)

    if nk * tk == seq_len:
        step(False)
    elif nk == 1:
        step(True)
    else:
        pl.when(ik < nk - 1)(lambda: step(False))
        pl.when(ik == nk - 1)(lambda: step(True))

    @pl.when(ik == nk - 1)
    def _():
        lp = lam_ref[...]
        a1 = jnp.sum(lp[0:1] * lp[1:2], axis=-1, keepdims=True)
        a2 = jnp.sum(lp[2:3] * lp[3:4], axis=-1, keepdims=True)
        lam = jnp.exp(a1) - jnp.exp(a2) + lam_init

        def finish(t):
            base = t * sub if nsub == 1 else pl.multiple_of(t * sub, sub)
            num0, num1 = acc_ref[t, 0, :ATT_V_DIM, :], acc_ref[t, 1, :ATT_V_DIM, :]
            den0, den1 = acc_ref[t, 0, ATT_V_DIM:ATT_V_DIM + 1, :], acc_ref[t, 1, ATT_V_DIM:ATT_V_DIM + 1, :]
            ot = num0 / den0 - lam * (num1 / den1)
            ms = jnp.mean(ot * ot, axis=0, keepdims=True)
            ot = ot * (lax.rsqrt(ms + RMS_EPS) * (1.0 - lam_init))
            o_ref[pl.ds(base, sub), :] = (ot.T * sn_ref[...]).astype(BF16)

        for_each_sub_tile(finish)


def _attn(q, k, v, layer, seq_len, plan, prep, lam_init):
    b, lp, _ = q.shape
    tq, tk, sub = plan["tq"], plan["tk"], plan["sub"]
    nk = lp // tk
    nsub = tq // sub
    return pl.pallas_call(
        functools.partial(_attn_kernel, seq_len, sub, nsub, tk, ATT_QUERY_BLOCK, nk, lam_init),
        grid=(b, ATT_HEADS, lp // tq, nk),
        in_specs=[
            pl.BlockSpec((None, tq, ATT_V_DIM), lambda bb, hh, iq, ik: (bb, iq, hh)),
            pl.BlockSpec((None, tk, ATT_V_DIM), lambda bb, hh, iq, ik: (bb, ik, hh)),
            pl.BlockSpec((None, tk, ATT_V_DIM), lambda bb, hh, iq, ik: (bb, ik, hh)),
            pl.BlockSpec((None, 4, ATT_HEAD_DIM), lambda bb, hh, iq, ik: (layer, 0, 0)),
            pl.BlockSpec((None, 1, ATT_V_DIM), lambda bb, hh, iq, ik: (layer, 0, 0)),
        ],
        out_specs=pl.BlockSpec((None, tq, ATT_V_DIM), lambda bb, hh, iq, ik: (bb, iq, hh)),
        out_shape=jax.ShapeDtypeStruct((b, lp, ATT_WIDTH), BF16),
        scratch_shapes=[pltpu.VMEM((nsub, 2, 1, sub), F32), pltpu.VMEM((nsub, 2, ATT_ACC_ROWS, sub), F32)],
        compiler_params=_params(("parallel", "parallel", "parallel", "arbitrary")),
        name="attn",
    )(q, k, v, prep["lam"], prep["sub_norm"])


def _gla_kernel(reverse, nchunk, q_ref, k_ref, v_ref, g_ref, o_ref, st_ref):
    c_rows = GLA_CHUNK

    @pl.when(pl.program_id(1) == 0)
    def _():
        st_ref[...] = jnp.zeros_like(st_ref)

    def before(a, b_):
        return (a >= b_) if reverse else (a <= b_)

    ti = lax.broadcasted_iota(jnp.int32, (2 * c_rows, c_rows), 0)
    si = lax.broadcasted_iota(jnp.int32, (2 * c_rows, c_rows), 1)
    t_loc = ti % c_rows
    incl = before(si, t_loc)
    blk = jnp.logical_and(before(si // GLA_SUB, t_loc // GLA_SUB), (si // GLA_SUB) != (t_loc // GLA_SUB))
    cum_mat = jnp.where(ti < c_rows, jnp.where(incl, 1.0, 0.0), jnp.where(blk, 1.0, 0.0)).astype(BF16)

    hr = lax.broadcasted_iota(jnp.int32, (GLA_WIDTH, GLA_WIDTH), 0) // GLA_HEAD_DIM
    hc = lax.broadcasted_iota(jnp.int32, (GLA_WIDTH, GLA_WIDTH), 1) // GLA_HEAD_DIM
    head_mask = hr == hc
    tt = lax.broadcasted_iota(jnp.int32, (c_rows, GLA_WIDTH), 0)
    ss = lax.broadcasted_iota(jnp.int32, (c_rows, GLA_WIDTH), 1) % c_rows
    causal = before(ss, tt)
    row_blk = lax.broadcasted_iota(jnp.int32, (c_rows, 1), 0) // GLA_SUB

    def rows_of(ci):
        c = (nchunk - 1 - ci) if reverse else ci
        return slice(c * c_rows, (c + 1) * c_rows)

    def stage_cumsum(ci):
        g_hi, g_lo = _split2(g_ref[rows_of(ci), :])
        return _dot(cum_mat, g_hi) + _dot(cum_mat, g_lo)

    def stage_intra(ci, cums):
        rows = rows_of(ci)
        q = q_ref[rows, :].astype(F32)
        k = k_ref[rows, :].astype(F32)
        b_incl = cums[:c_rows]
        b_ref = cums[c_rows:]
        b_last = b_incl[0:1] if reverse else b_incl[c_rows - 1:c_rows]
        qs = q * jnp.exp(b_incl - b_ref)
        qe = (q * jnp.exp(b_incl)).astype(BF16)
        kd = (k * jnp.exp(b_last - b_incl)).astype(BF16)
        lhs_parts, rhs_parts = [], []
        for i in range(GLA_NSUB):
            r_i = b_ref[GLA_SUB * i:GLA_SUB * i + 1]
            k_i = jnp.where(before(row_blk, i), k * jnp.exp(jnp.minimum(r_i - b_incl, GLA_EXP_CLAMP)), 0.0)
            k_i4 = jnp.concatenate([k_i] * GLA_HEADS, axis=0)
            rhs_parts.append(jnp.where(head_mask, k_i4, 0.0).astype(BF16))
            lhs_parts.append(jnp.where(row_blk == i, qs, 0.0).astype(BF16))
        lhs = jnp.concatenate(lhs_parts, axis=1)
        rhs = jnp.concatenate(rhs_parts, axis=1)
        a = jnp.where(causal, _dot_nt(lhs, rhs), 0.0).astype(BF16)
        return a, qe, kd, jnp.exp(b_last)

    def stage_state(ci, st, a, qe, kd, decay):
        rows = rows_of(ci)
        v = v_ref[rows, :]
        v4 = jnp.where(head_mask, jnp.concatenate([v] * GLA_HEADS, axis=0), jnp.zeros((), BF16))
        o_ref[rows, :] = _dot(a, v4) + _dot_nt(qe, st.astype(BF16))
        return st * decay + jnp.where(head_mask, _dot_tn(v, kd), 0.0)

    st = st_ref[...]
    cums, intra = {}, {}
    for step in range(nchunk + 2):
        if step < nchunk:
            cums[step] = stage_cumsum(step)
        if 0 <= step - 1 < nchunk:
            intra[step - 1] = stage_intra(step - 1, cums.pop(step - 1))
        if 0 <= step - 2 < nchunk:
            st = stage_state(step - 2, st, *intra.pop(step - 2))
    st_ref[...] = st


def _gla(gq, gk, gv, gdecay, plan, reverse):
    b, lp, _ = gq.shape
    tt = plan["gla_tile"]
    nt = lp // tt
    idx = (lambda bb, i: (bb, nt - 1 - i, 0)) if reverse else (lambda bb, i: (bb, i, 0))
    spec = pl.BlockSpec((None, tt, GLA_WIDTH), idx)
    return pl.pallas_call(
        functools.partial(_gla_kernel, reverse, tt // GLA_CHUNK),
        grid=(b, nt),
        in_specs=[spec, spec, spec, spec],
        out_specs=spec,
        out_shape=jax.ShapeDtypeStruct((b, lp, GLA_WIDTH), F32),
        scratch_shapes=[pltpu.VMEM((GLA_WIDTH, GLA_WIDTH), F32)],
        compiler_params=_params(("parallel", "arbitrary")),
        name="gla_bwd" if reverse else "gla_fwd",
    )(gq, gk, gv, gdecay)


def _post_kernel(seq_len, tm, nt, h_ref, oa_ref, pu_ref, pup_ref, pun_ref, of_ref, ob_ref, gg_ref,
                 pw_ref, ps_ref, gn_ref, bd_ref, wo_ref, o_ref):
    i = pl.program_id(1)
    cur = pu_ref[...]
    prev = jnp.where(i > 0, pup_ref[...], 0.0)
    nxt = jnp.where(i < nt - 1, pun_ref[...], 0.0)
    ext = jnp.concatenate([prev, cur, nxt], axis=0)
    n = tm + 2 * POOL_HALO
    p2 = ext + pltpu.roll(ext, 1, 0)
    p4 = p2 + pltpu.roll(p2, 2, 0)
    p8 = p4 + pltpu.roll(p4, 4, 0)
    p16 = p8 + pltpu.roll(p8, 8, 0)
    sums = (p2, pltpu.roll(p4, n - 1, 0), pltpu.roll(p8, n - 3, 0), pltpu.roll(p16, n - 7, 0))

    pos = i * tm + lax.broadcasted_iota(jnp.int32, (tm, 1), 0)
    grp = lax.broadcasted_iota(jnp.int32, (tm, POOL_WIDTH), 1) // POOL_GROUP_DIM
    wsum = jnp.zeros((tm, POOL_WIDTH), F32)
    cnt = jnp.ones((tm, POOL_WIDTH), F32)
    for gi, win in enumerate(POOL_WINDOWS):
        lo = win // 2
        hi = win - lo - 1
        c = jnp.minimum(pos + hi + 1, seq_len) - jnp.maximum(pos - lo, 0)
        c = jnp.maximum(c, 1).astype(F32)
        sel = grp == gi
        wsum = jnp.where(sel, sums[gi][POOL_HALO:POOL_HALO + tm], wsum)
        cnt = jnp.where(sel, c, cnt)
    pooled = (wsum / cnt - cur).astype(BF16)
    y_pool = (_dot(pooled, pw_ref[...]) * ps_ref[...]).astype(BF16)

    og = of_ref[...] + ob_ref[...]
    og = og * lax.rsqrt(_group_mean_sq(og, bd_ref[...]) + RMS_EPS) * gn_ref[...]
    gate = gg_ref[...]
    y_gla = (og * (gate * jax.nn.sigmoid(gate))).astype(BF16)

    mixed = (_dot(oa_ref[...], wo_ref[0:ATT_WIDTH, :])
             + _dot(y_pool, wo_ref[ATT_WIDTH:ATT_WIDTH + POOL_WIDTH, :])
             + _dot(y_gla, wo_ref[ATT_WIDTH + POOL_WIDTH:MIX_WIDTH, :]))
    o_ref[...] = h_ref[...] + mixed


def _post(h, o_att, pu, o_f, o_b, gg, layer, seq_len, plan, prep):
    b, lp, _ = h.shape
    tm = plan["row_tile"]
    nt = lp // tm
    hb = tm // POOL_HALO
    row = lambda w: pl.BlockSpec((None, tm, w), lambda bb, i: (bb, i, 0))
    par = lambda *shape: pl.BlockSpec((None,) + shape, lambda bb, i: (layer,) + (0,) * len(shape))
    return pl.pallas_call(
        functools.partial(_post_kernel, seq_len, tm, nt),
        grid=(b, nt),
        in_specs=[
            row(D_MODEL),
            row(ATT_WIDTH),
            row(POOL_WIDTH),
            pl.BlockSpec((None, POOL_HALO, POOL_WIDTH), lambda bb, i: (bb, jnp.maximum(i * hb - 1, 0), 0)),
            pl.BlockSpec((None, POOL_HALO, POOL_WIDTH),
                         lambda bb, i: (bb, jnp.minimum((i + 1) * hb, lp // POOL_HALO - 1), 0)),
            row(GLA_WIDTH),
            row(GLA_WIDTH),
            row(GLA_WIDTH),
            par(POOL_WIDTH, POOL_WIDTH),
            par(1, POOL_WIDTH),
            par(1, GLA_WIDTH),
            pl.BlockSpec((V7X_MXU_DIM, V7X_MXU_DIM), lambda bb, i: (0, 0)),
            par(MIX_WIDTH, D_MODEL),
        ],
        out_specs=row(D_MODEL),
        out_shape=jax.ShapeDtypeStruct((b, lp, D_MODEL), F32),
        compiler_params=_params(("parallel", "parallel")),
        name="post",
    )(h, o_att, pu, pu, pu, o_f, o_b, gg, prep["pool_w"], prep["pool_scale"], prep["gla_norm"], prep["bd"],
      prep["w_out"])


def _rope_tables(lp):
    pos = jnp.arange(lp, dtype=F32)
    inv_freq = ROPE_THETA ** (-jnp.arange(0, ATT_HEAD_DIM, 2, dtype=F32) / ATT_HEAD_DIM)
    ang = pos[:, None] * inv_freq[None, :]
    cos, sin = jnp.cos(ang), jnp.sin(ang)
    return jnp.tile(cos, (1, 4)), jnp.tile(jnp.concatenate([-sin, sin], axis=1), (1, 2))


def _prepare(p, padded_lengths):
    eye = jnp.eye(V7X_MXU_DIM // ATT_HEAD_DIM, dtype=F32)
    bd = jnp.kron(eye, jnp.full((ATT_HEAD_DIM, ATT_HEAD_DIM), 1.0 / ATT_HEAD_DIM, F32)).astype(BF16)
    w_in = jnp.pad(p["w_in"], ((0, 0), (0, 0), (0, IN_COLS_PAD - IN_COLS))).astype(BF16)
    gu = jnp.zeros((DEPTH, V7X_LANES, 2 * GLA_WIDTH), F32)
    gu = gu.at[:, :GLA_GATE_RANK, :GLA_WIDTH].set(p["gla_gate_up"][:, 0])
    gu = gu.at[:, GLA_GATE_RANK:2 * GLA_GATE_RANK, GLA_WIDTH:].set(p["gla_gate_up"][:, 1])
    pool_w = jnp.zeros((DEPTH, POOL_WIDTH, POOL_WIDTH), F32)
    for gi in range(len(POOL_WINDOWS)):
        sl = slice(gi * POOL_GROUP_DIM, (gi + 1) * POOL_GROUP_DIM)
        pool_w = pool_w.at[:, sl, sl].set(p["pool_w"][:, gi])
    tables = {lp: _rope_tables(lp) for lp in padded_lengths}
    return dict(
        bd=bd,
        w_in=w_in,
        w_out=p["w_out"].astype(BF16),
        mix_norm=p["mix_norm"][:, None, :],
        q_gain=jnp.tile(p["attn_q_norm"], (1, 2))[:, None, :],
        k_gain=jnp.tile(p["attn_k_norm"], (1, 2))[:, None, :],
        lam=jnp.stack([p["lambda_q1"], p["lambda_k1"], p["lambda_q2"], p["lambda_k2"]], axis=1),
        sub_norm=p["attn_sub_norm"][:, None, :],
        gate_up=gu,
        gate_bias=p["gla_gate_bias"].reshape(DEPTH, 1, 2 * GLA_WIDTH),
        pool_w=pool_w.astype(BF16),
        pool_scale=p["pool_scale"][:, None, :],
        gla_norm=jnp.tile(p["gla_out_norm"], (1, GLA_HEADS))[:, None, :],
        ffn1_norm=p["ffn1_norm"][:, None, :],
        ffn2_norm=p["ffn2_norm"][:, None, :],
        ffn1_w_in=p["ffn1_w_in"].astype(BF16),
        ffn1_w_out=p["ffn1_w_out"].astype(BF16),
        ffn2_w_in=p["ffn2_w_in"].astype(BF16),
        ffn2_w_out=p["ffn2_w_out"].astype(BF16),
        cos={lp: t[0] for lp, t in tables.items()},
        sin={lp: t[1] for lp, t in tables.items()},
    )


def _trunk(x, meta_tokens, prep):
    b, s, _ = x.shape
    seq_len = s + N_META
    plan = _plan(seq_len)
    lp = plan["lp"]
    meta = jnp.broadcast_to(meta_tokens.astype(x.dtype)[None], (b, N_META, D_MODEL))
    h = jnp.concatenate([meta, x, jnp.zeros((b, lp - seq_len, D_MODEL), x.dtype)], axis=1)
    for layer in range(DEPTH):
        lam_init = 0.8 - 0.6 * math.exp(-0.3 * layer)
        h = _ffn(h.reshape(b * lp, D_MODEL), layer, prep["ffn1_norm"], prep["ffn1_w_in"],
                 prep["ffn1_w_out"]).reshape(b, lp, D_MODEL)
        q, k, v, pu, gq, gk, gv, gg, gf, gb = _proj(h, layer, seq_len, plan, prep)
        o_att = _attn(q, k, v, layer, seq_len, plan, prep, lam_init)
        o_f = _gla(gq, gk, gv, gf, plan, reverse=False)
        o_b = _gla(gq, gk, gv, gb, plan, reverse=True)
        h = _post(h, o_att, pu, o_f, o_b, gg, layer, seq_len, plan, prep)
        h = _ffn(h.reshape(b * lp, D_MODEL), layer, prep["ffn2_norm"], prep["ffn2_w_in"],
                 prep["ffn2_w_out"]).reshape(b, lp, D_MODEL)
    return h[:, N_META:seq_len]


def kernel(x_prompt, x_sample, meta_tokens, ffn1_norm, ffn1_w_in, ffn1_w_out, mix_norm, w_in, w_out, attn_q_norm,
           attn_k_norm, lambda_q1, lambda_k1, lambda_q2, lambda_k2, attn_sub_norm, pool_w, pool_scale, gla_gate_up,
           gla_gate_bias, gla_out_norm, ffn2_norm, ffn2_w_in, ffn2_w_out):
    p = dict(ffn1_norm=ffn1_norm, ffn1_w_in=ffn1_w_in, ffn1_w_out=ffn1_w_out, mix_norm=mix_norm, w_in=w_in,
             w_out=w_out, attn_q_norm=attn_q_norm, attn_k_norm=attn_k_norm, lambda_q1=lambda_q1,
             lambda_k1=lambda_k1, lambda_q2=lambda_q2, lambda_k2=lambda_k2, attn_sub_norm=attn_sub_norm,
             pool_w=pool_w, pool_scale=pool_scale, gla_gate_up=gla_gate_up, gla_gate_bias=gla_gate_bias,
             gla_out_norm=gla_out_norm, ffn2_norm=ffn2_norm, ffn2_w_in=ffn2_w_in, ffn2_w_out=ffn2_w_out)
    lengths = {_plan(x.shape[1] + N_META)["lp"] for x in (x_prompt, x_sample)}
    prep = _prepare(p, lengths)
    return (_trunk(x_prompt, meta_tokens, prep), _trunk(x_sample, meta_tokens, prep))
```

```python
import functools
import math

import jax
import jax.numpy as jnp
from jax import lax
from jax.experimental import pallas as pl
from jax.experimental.pallas import tpu as pltpu

F32 = jnp.float32
BF16 = jnp.bfloat16

D_MODEL = 1024
D_FF = 2816
DEPTH = 4
N_META = 16
RMS_EPS = 1e-6
ROPE_THETA = 10000.0

ATT_HEADS = 4
ATT_HEAD_DIM = 64
ATT_V_DIM = 2 * ATT_HEAD_DIM
ATT_WIDTH = ATT_HEADS * ATT_V_DIM
ATT_ACC_ROWS = ATT_V_DIM + 16

POOL_WINDOWS = (2, 4, 8, 16)
POOL_GROUP_DIM = 64
POOL_WIDTH = len(POOL_WINDOWS) * POOL_GROUP_DIM
POOL_HALO = 8

GLA_HEADS = 4
GLA_HEAD_DIM = 64
GLA_WIDTH = GLA_HEADS * GLA_HEAD_DIM
GLA_GATE_RANK = 16
GLA_GATE_TEMP = 16.0
GLA_CHUNK = 64
GLA_SUB = 16
GLA_NSUB = GLA_CHUNK // GLA_SUB
GLA_EXP_CLAMP = 80.0

MIX_WIDTH = ATT_WIDTH + POOL_WIDTH + GLA_WIDTH
IN_COLS = 4 * ATT_WIDTH - ATT_WIDTH + POOL_WIDTH + 4 * GLA_WIDTH + 2 * GLA_GATE_RANK
GATE_COL0 = IN_COLS - 2 * GLA_GATE_RANK
IN_COLS_PAD = GATE_COL0 + 128

V7X_LANES = 128
V7X_MXU_DIM = 256
SEQ_ALIGN = 128
SHORT_SEQ_ROWS = 2304
ATT_Q_ROWS_CAP = 16640
ATT_KV_TILE_LONG = 1280
ATT_QUERY_BLOCK = V7X_MXU_DIM
ATT_KEY_BLOCK = V7X_MXU_DIM
ATT_SKEW_RUNNING_MAX = 5
ATT_SKEW_DIRECT = 2
ATT_DIRECT_BOUND = 90.0
ROW_TILE_CAP = 704
GLA_TILE_CAP = 1088
FFN_TILE_CAP = 704
VMEM_LIMIT_BYTES = 56 * 1024 * 1024
FF_TILE = 256
LOG2E = 1.4426950408889634


def _divisor_tile(n, cap, mult):
    best = None
    for t in range(mult, min(n, cap) + 1, mult):
        if n % t == 0:
            best = t
    assert best is not None, (n, cap, mult)
    return best


def _plan(seq_len):
    if seq_len <= SHORT_SEQ_ROWS:
        lp = -(-seq_len // SEQ_ALIGN) * SEQ_ALIGN
        tk = tq = sub = lp
    else:
        tk = sub = ATT_KV_TILE_LONG
        lp = -(-seq_len // tk) * tk
        tq = _divisor_tile(lp, ATT_Q_ROWS_CAP, sub)
    return dict(
        lp=lp,
        row_tile=_divisor_tile(lp, ROW_TILE_CAP, 16),
        tq=tq,
        tk=tk,
        sub=sub,
        gla_tile=_divisor_tile(lp, GLA_TILE_CAP, GLA_CHUNK),
    )


def _params(sem):
    return pltpu.CompilerParams(dimension_semantics=sem, vmem_limit_bytes=VMEM_LIMIT_BYTES)


def _dot(a, b):
    return jnp.dot(a, b, preferred_element_type=F32)


def _dot_nt(a, b):
    return lax.dot_general(a, b, (((1,), (1,)), ((), ())), preferred_element_type=F32)


def _dot_tn(a, b):
    return lax.dot_general(a, b, (((0,), (0,)), ((), ())), preferred_element_type=F32)


def _split2(x):
    hi = x.astype(BF16)
    lo = (x - hi.astype(F32)).astype(BF16)
    return hi, lo


def _group_mean_sq(x, bd):
    hi, lo = _split2(x * x)
    outs = []
    for c in range(x.shape[1] // V7X_MXU_DIM):
        sl = slice(V7X_MXU_DIM * c, V7X_MXU_DIM * (c + 1))
        outs.append(_dot(hi[:, sl], bd) + _dot(lo[:, sl], bd))
    return outs[0] if len(outs) == 1 else jnp.concatenate(outs, axis=1)


def _ffn_kernel(x_ref, g_ref, wi_ref, wo_ref, o_ref):
    x = x_ref[...]
    ms = jnp.mean(x * x, axis=-1, keepdims=True)
    xn = (x * lax.rsqrt(ms + RMS_EPS) * g_ref[...]).astype(BF16)
    acc = None
    for c0 in range(0, D_FF, FF_TILE):
        g = _dot(xn, wi_ref[:, c0:c0 + FF_TILE])
        u = _dot(xn, wi_ref[:, D_FF + c0:D_FF + c0 + FF_TILE])
        a = (g * jax.nn.sigmoid(g) * u).astype(BF16)
        d = _dot(a, wo_ref[c0:c0 + FF_TILE, :])
        acc = d if acc is None else acc + d
    o_ref[...] = x + 0.5 * acc


def _ffn(h2d, layer, norm, w_in, w_out):
    rows = h2d.shape[0]
    tm = _divisor_tile(rows, FFN_TILE_CAP, 16)
    resident = lambda *shape: pl.BlockSpec((None,) + shape, lambda i: (layer,) + (0,) * len(shape),
                                           pipeline_mode=pl.Buffered(1))
    return pl.pallas_call(
        _ffn_kernel,
        grid=(rows // tm,),
        in_specs=[
            pl.BlockSpec((tm, D_MODEL), lambda i: (i, 0)),
            pl.BlockSpec((None, 1, D_MODEL), lambda i: (layer, 0, 0)),
            resident(D_MODEL, 2 * D_FF),
            resident(D_FF, D_MODEL),
        ],
        out_specs=pl.BlockSpec((tm, D_MODEL), lambda i: (i, 0)),
        out_shape=jax.ShapeDtypeStruct((rows, D_MODEL), F32),
        compiler_params=_params(("parallel",)),
        name="ffn",
    )(h2d, norm, w_in, w_out)


def _rope128(x, cos, sin_signed, first_half):
    partner = jnp.where(first_half, pltpu.roll(x, 96, 1), pltpu.roll(x, 32, 1))
    return x * cos + partner * sin_signed


def _proj_kernel(seq_len, tm, h_ref, g_ref, w_ref, cos_ref, sin_ref, qg_ref, kg_ref, bd_ref, gu_ref, gbias_ref,
                 q_ref, k_ref, v_ref, pu_ref, gq_ref, gk_ref, gv_ref, gg_ref, gf_ref, gb_ref):
    i = pl.program_id(1)
    x = h_ref[...]
    pos = i * tm + lax.broadcasted_iota(jnp.int32, (tm, 1), 0)
    valid = pos < seq_len
    ms = jnp.mean(x * x, axis=-1, keepdims=True)
    xn = jnp.where(valid, x * lax.rsqrt(ms + RMS_EPS) * g_ref[...], 0.0).astype(BF16)

    bd = bd_ref[...]
    cos = cos_ref[...]
    sin = sin_ref[...]
    lane = lax.broadcasted_iota(jnp.int32, (tm, V7X_LANES), 1)
    first_half = (lane % ATT_HEAD_DIM) < (ATT_HEAD_DIM // 2)

    def qk_path(col0, gain_ref, out_ref, out_scale):
        z = _dot(xn, w_ref[:, col0:col0 + ATT_WIDTH])
        zn = z * lax.rsqrt(_group_mean_sq(z, bd) + RMS_EPS)
        gain = gain_ref[...]
        for c in range(ATT_WIDTH // V7X_LANES):
            sl = slice(V7X_LANES * c, V7X_LANES * (c + 1))
            r = _rope128(zn[:, sl] * gain, cos, sin, first_half)
            out_ref[:, sl] = (r * out_scale).astype(BF16)

    qk_path(0, qg_ref, q_ref, ATT_HEAD_DIM ** -0.5 * LOG2E)
    qk_path(ATT_WIDTH, kg_ref, k_ref, 1.0)

    v_ref[...] = _dot(xn, w_ref[:, 2 * ATT_WIDTH:3 * ATT_WIDTH]).astype(BF16)

    c0 = 3 * ATT_WIDTH
    pu_ref[...] = _dot(xn, w_ref[:, c0:c0 + POOL_WIDTH])
    c0 += POOL_WIDTH
    gq_ref[...] = (_dot(xn, w_ref[:, c0:c0 + GLA_WIDTH]) * (GLA_HEAD_DIM ** -0.5)).astype(BF16)
    c0 += GLA_WIDTH
    gk_ref[...] = _dot(xn, w_ref[:, c0:c0 + GLA_WIDTH]).astype(BF16)
    c0 += GLA_WIDTH
    gv_ref[...] = _dot(xn, w_ref[:, c0:c0 + GLA_WIDTH]).astype(BF16)
    c0 += GLA_WIDTH
    gg_ref[...] = _dot(xn, w_ref[:, c0:c0 + GLA_WIDTH])

    glr = _dot(xn, w_ref[:, GATE_COL0:IN_COLS_PAD])
    glr_hi, glr_lo = _split2(glr)
    gu_hi, gu_lo = _split2(gu_ref[...])
    logit = _dot(glr_hi, gu_hi) + _dot(glr_lo, gu_hi) + _dot(glr_hi, gu_lo) + gbias_ref[...]
    log_sig = jnp.minimum(logit, 0.0) - jnp.log(1.0 + jnp.exp(-jnp.abs(logit)))
    log_a = log_sig * (1.0 / GLA_GATE_TEMP)
    gf_ref[...] = log_a[:, :GLA_WIDTH]
    gb_ref[...] = log_a[:, GLA_WIDTH:]


def _proj(h, layer, seq_len, plan, prep):
    b, lp, _ = h.shape
    tm = plan["row_tile"]
    row = lambda w: pl.BlockSpec((None, tm, w), lambda bb, i: (bb, i, 0))
    par = lambda *shape: pl.BlockSpec((None,) + shape, lambda bb, i: (layer,) + (0,) * len(shape))
    out_widths = [(ATT_WIDTH, BF16), (ATT_WIDTH, BF16), (ATT_WIDTH, BF16), (POOL_WIDTH, F32),
                  (GLA_WIDTH, BF16), (GLA_WIDTH, BF16), (GLA_WIDTH, BF16), (GLA_WIDTH, F32),
                  (GLA_WIDTH, F32), (GLA_WIDTH, F32)]
    return pl.pallas_call(
        functools.partial(_proj_kernel, seq_len, tm),
        grid=(b, lp // tm),
        in_specs=[
            row(D_MODEL),
            par(1, D_MODEL),
            par(D_MODEL, IN_COLS_PAD),
            pl.BlockSpec((tm, V7X_LANES), lambda bb, i: (i, 0)),
            pl.BlockSpec((tm, V7X_LANES), lambda bb, i: (i, 0)),
            par(1, V7X_LANES),
            par(1, V7X_LANES),
            pl.BlockSpec((V7X_MXU_DIM, V7X_MXU_DIM), lambda bb, i: (0, 0)),
            par(V7X_LANES, 2 * GLA_WIDTH),
            par(1, 2 * GLA_WIDTH),
        ],
        out_specs=[row(w) for w, _ in out_widths],
        out_shape=[jax.ShapeDtypeStruct((b, lp, w), dt) for w, dt in out_widths],
        compiler_params=_params(("parallel", "parallel")),
        name="proj",
    )(h, prep["mix_norm"], prep["w_in"], prep["cos"][lp], prep["sin"][lp], prep["q_gain"], prep["k_gain"],
      prep["bd"], prep["gate_up"], prep["gate_bias"])


def _attn_kernel(seq_len, sub, nsub, tk, qb, nk, layer, lam_init, bound_ref, q_ref, k_ref, v_ref, lam_ref, sn_ref,
                 o_ref, m_ref, acc_ref):
    ik = pl.program_id(3)
    direct = bound_ref[layer] <= ATT_DIRECT_BOUND

    def for_each_sub_tile(body):
        if nsub == 1:
            body(0)
        else:
            lax.fori_loop(0, nsub, lambda t, carry: (body(t), carry)[1], 0)

    @pl.when(ik == 0)
    def _():
        m_ref[...] = jnp.full_like(m_ref, -jnp.inf)
        acc_ref[...] = jnp.zeros_like(acc_ref)

    def step(mask_tail, running_max):
        k = k_ref[...]
        ones_rows = jnp.where(lax.broadcasted_iota(jnp.int32, (ATT_ACC_ROWS - ATT_V_DIM, tk), 0) == 0, 1.0, 0.0)
        vt = jnp.concatenate([v_ref[...].T, ones_rows.astype(BF16)], axis=0)
        lane = lax.broadcasted_iota(jnp.int32, k.shape, 1)
        halves = (jnp.where(lane < ATT_HEAD_DIM, k, jnp.zeros_like(k)),
                  jnp.where(lane >= ATT_HEAD_DIM, k, jnp.zeros_like(k)))
        if mask_tail:
            key_ok = (ik * tk + lax.broadcasted_iota(jnp.int32, (tk, 1), 0)) < seq_len
        kb = _divisor_tile(tk, ATT_KEY_BLOCK, V7X_MXU_DIM) if (running_max and tk % V7X_MXU_DIM == 0) else tk
        skew = ATT_SKEW_RUNNING_MAX if running_max else ATT_SKEW_DIRECT
        chains = [(slice(r0, r0 + kb), c0, min(qb, sub - c0), c)
                  for r0 in range(0, tk, kb) for c0 in range(0, sub, qb) for c in (0, 1)]

        def sub_tile(t):
            base = t * sub if nsub == 1 else pl.multiple_of(t * sub, sub)

            def scores(keys, c0, width, c):
                return _dot_nt(halves[c][keys, :], q_ref[pl.ds(base + c0, width), :])

            def softmax_update(keys, c0, width, c, st):
                cols = slice(c0, c0 + width)
                if mask_tail:
                    st = jnp.where(key_ok[keys, :], st, -jnp.inf)
                if not running_max:
                    acc_ref[t, c, :, cols] += _dot(vt[:, keys], jnp.exp2(st).astype(BF16))
                    return
                m_prev = m_ref[t, c, :, cols]
                m_new = jnp.maximum(m_prev, jnp.max(st, axis=0, keepdims=True))
                alpha = jnp.exp2(m_prev - m_new)
                pt = jnp.exp2(st - m_new)
                acc_ref[t, c, :, cols] = alpha * acc_ref[t, c, :, cols] + _dot(vt[:, keys], pt.astype(BF16))
                m_ref[t, c, :, cols] = m_new

            pending = {}
            for i in range(len(chains) + skew):
                if i < len(chains):
                    pending[i] = scores(*chains[i])
                if i >= skew:
                    softmax_update(*chains[i - skew], pending.pop(i - skew))

        for_each_sub_tile(sub_tile)

    def step_either_form(mask_tail):
        pl.when(direct)(lambda: step(mask_tail, running_max=False))
        pl.when(jnp.logical_not(direct))(lambda: step(mask_tail, running_max=True))

    if nk * tk == seq_len:
        step_either_form(False)
    elif nk == 1:
        step_either_form(True)
    else:
        pl.when(ik < nk - 1)(lambda: step_either_form(False))
        pl.when(ik == nk - 1)(lambda: step_either_form(True))

    @pl.when(ik == nk - 1)
    def _():
        lp = lam_ref[...]
        a1 = jnp.sum(lp[0:1] * lp[1:2], axis=-1, keepdims=True)
        a2 = jnp.sum(lp[2:3] * lp[3:4], axis=-1, keepdims=True)
        lam = jnp.exp(a1) - jnp.exp(a2) + lam_init

        def finish(t):
            base = t * sub if nsub == 1 else pl.multiple_of(t * sub, sub)
            num0, num1 = acc_ref[t, 0, :ATT_V_DIM, :], acc_ref[t, 1, :ATT_V_DIM, :]
            den0, den1 = acc_ref[t, 0, ATT_V_DIM:ATT_V_DIM + 1, :], acc_ref[t, 1, ATT_V_DIM:ATT_V_DIM + 1, :]
            ot = num0 / den0 - lam * (num1 / den1)
            ms = jnp.mean(ot * ot, axis=0, keepdims=True)
            ot = ot * (lax.rsqrt(ms + RMS_EPS) * (1.0 - lam_init))
            o_ref[pl.ds(base, sub), :] = (ot.T * sn_ref[...]).astype(BF16)

        for_each_sub_tile(finish)


def _attn(q, k, v, layer, seq_len, plan, prep, lam_init):
    b, lp, _ = q.shape
    tq, tk, sub = plan["tq"], plan["tk"], plan["sub"]
    nk = lp // tk
    nsub = tq // sub
    return pl.pallas_call(
        functools.partial(_attn_kernel, seq_len, sub, nsub, tk, ATT_QUERY_BLOCK, nk, layer, lam_init),
        grid=(b, ATT_HEADS, lp // tq, nk),
        in_specs=[
            pl.BlockSpec(memory_space=pltpu.SMEM),
            pl.BlockSpec((None, tq, ATT_V_DIM), lambda bb, hh, iq, ik: (bb, iq, hh)),
            pl.BlockSpec((None, tk, ATT_V_DIM), lambda bb, hh, iq, ik: (bb, ik, hh)),
            pl.BlockSpec((None, tk, ATT_V_DIM), lambda bb, hh, iq, ik: (bb, ik, hh)),
            pl.BlockSpec((None, 4, ATT_HEAD_DIM), lambda bb, hh, iq, ik: (layer, 0, 0)),
            pl.BlockSpec((None, 1, ATT_V_DIM), lambda bb, hh, iq, ik: (layer, 0, 0)),
        ],
        out_specs=pl.BlockSpec((None, tq, ATT_V_DIM), lambda bb, hh, iq, ik: (bb, iq, hh)),
        out_shape=jax.ShapeDtypeStruct((b, lp, ATT_WIDTH), BF16),
        scratch_shapes=[pltpu.VMEM((nsub, 2, 1, sub), F32), pltpu.VMEM((nsub, 2, ATT_ACC_ROWS, sub), F32)],
        compiler_params=_params(("parallel", "parallel", "parallel", "arbitrary")),
        name="attn",
    )(prep["score_bound"], q, k, v, prep["lam"], prep["sub_norm"])


def _gla_kernel(reverse, nchunk, q_ref, k_ref, v_ref, g_ref, o_ref, st_ref):
    c_rows = GLA_CHUNK

    @pl.when(pl.program_id(1) == 0)
    def _():
        st_ref[...] = jnp.zeros_like(st_ref)

    def before(a, b_):
        return (a >= b_) if reverse else (a <= b_)

    ti = lax.broadcasted_iota(jnp.int32, (2 * c_rows, c_rows), 0)
    si = lax.broadcasted_iota(jnp.int32, (2 * c_rows, c_rows), 1)
    t_loc = ti % c_rows
    incl = before(si, t_loc)
    blk = jnp.logical_and(before(si // GLA_SUB, t_loc // GLA_SUB), (si // GLA_SUB) != (t_loc // GLA_SUB))
    cum_mat = jnp.where(ti < c_rows, jnp.where(incl, 1.0, 0.0), jnp.where(blk, 1.0, 0.0)).astype(BF16)

    hr = lax.broadcasted_iota(jnp.int32, (GLA_WIDTH, GLA_WIDTH), 0) // GLA_HEAD_DIM
    hc = lax.broadcasted_iota(jnp.int32, (GLA_WIDTH, GLA_WIDTH), 1) // GLA_HEAD_DIM
    head_mask = hr == hc
    tt = lax.broadcasted_iota(jnp.int32, (c_rows, GLA_WIDTH), 0)
    ss = lax.broadcasted_iota(jnp.int32, (c_rows, GLA_WIDTH), 1) % c_rows
    causal = before(ss, tt)
    row_blk = lax.broadcasted_iota(jnp.int32, (c_rows, 1), 0) // GLA_SUB

    def rows_of(ci):
        c = (nchunk - 1 - ci) if reverse else ci
        return slice(c * c_rows, (c + 1) * c_rows)

    def stage_cumsum(ci):
        g_hi, g_lo = _split2(g_ref[rows_of(ci), :])
        return _dot(cum_mat, g_hi) + _dot(cum_mat, g_lo)

    def stage_intra(ci, cums):
        rows = rows_of(ci)
        q = q_ref[rows, :].astype(F32)
        k = k_ref[rows, :].astype(F32)
        b_incl = cums[:c_rows]
        b_ref = cums[c_rows:]
        b_last = b_incl[0:1] if reverse else b_incl[c_rows - 1:c_rows]
        qs = q * jnp.exp(b_incl - b_ref)
        qe = (q * jnp.exp(b_incl)).astype(BF16)
        kd = (k * jnp.exp(b_last - b_incl)).astype(BF16)
        lhs_parts, rhs_parts = [], []
        for i in range(GLA_NSUB):
            r_i = b_ref[GLA_SUB * i:GLA_SUB * i + 1]
            k_i = jnp.where(before(row_blk, i), k * jnp.exp(jnp.minimum(r_i - b_incl, GLA_EXP_CLAMP)), 0.0)
            k_i4 = jnp.concatenate([k_i] * GLA_HEADS, axis=0)
            rhs_parts.append(jnp.where(head_mask, k_i4, 0.0).astype(BF16))
            lhs_parts.append(jnp.where(row_blk == i, qs, 0.0).astype(BF16))
        lhs = jnp.concatenate(lhs_parts, axis=1)
        rhs = jnp.concatenate(rhs_parts, axis=1)
        a = jnp.where(causal, _dot_nt(lhs, rhs), 0.0).astype(BF16)
        return a, qe, kd, jnp.exp(b_last)

    def stage_state(ci, st, a, qe, kd, decay):
        rows = rows_of(ci)
        v = v_ref[rows, :]
        v4 = jnp.where(head_mask, jnp.concatenate([v] * GLA_HEADS, axis=0), jnp.zeros((), BF16))
        o_ref[rows, :] = _dot(a, v4) + _dot_nt(qe, st.astype(BF16))
        return st * decay + jnp.where(head_mask, _dot_tn(v, kd), 0.0)

    st = st_ref[...]
    cums, intra = {}, {}
    for step in range(nchunk + 2):
        if step < nchunk:
            cums[step] = stage_cumsum(step)
        if 0 <= step - 1 < nchunk:
            intra[step - 1] = stage_intra(step - 1, cums.pop(step - 1))
        if 0 <= step - 2 < nchunk:
            st = stage_state(step - 2, st, *intra.pop(step - 2))
    st_ref[...] = st


def _gla(gq, gk, gv, gdecay, plan, reverse):
    b, lp, _ = gq.shape
    tt = plan["gla_tile"]
    nt = lp // tt
    idx = (lambda bb, i: (bb, nt - 1 - i, 0)) if reverse else (lambda bb, i: (bb, i, 0))
    spec = pl.BlockSpec((None, tt, GLA_WIDTH), idx)
    return pl.pallas_call(
        functools.partial(_gla_kernel, reverse, tt // GLA_CHUNK),
        grid=(b, nt),
        in_specs=[spec, spec, spec, spec],
        out_specs=spec,
        out_shape=jax.ShapeDtypeStruct((b, lp, GLA_WIDTH), F32),
        scratch_shapes=[pltpu.VMEM((GLA_WIDTH, GLA_WIDTH), F32)],
        compiler_params=_params(("parallel", "arbitrary")),
        name="gla_bwd" if reverse else "gla_fwd",
    )(gq, gk, gv, gdecay)


def _post_kernel(seq_len, tm, nt, h_ref, oa_ref, pu_ref, pup_ref, pun_ref, of_ref, ob_ref, gg_ref,
                 pw_ref, ps_ref, gn_ref, bd_ref, wo_ref, o_ref):
    i = pl.program_id(1)
    cur = pu_ref[...]
    prev = jnp.where(i > 0, pup_ref[...], 0.0)
    nxt = jnp.where(i < nt - 1, pun_ref[...], 0.0)
    ext = jnp.concatenate([prev, cur, nxt], axis=0)
    n = tm + 2 * POOL_HALO
    p2 = ext + pltpu.roll(ext, 1, 0)
    p4 = p2 + pltpu.roll(p2, 2, 0)
    p8 = p4 + pltpu.roll(p4, 4, 0)
    p16 = p8 + pltpu.roll(p8, 8, 0)
    sums = (p2, pltpu.roll(p4, n - 1, 0), pltpu.roll(p8, n - 3, 0), pltpu.roll(p16, n - 7, 0))

    pos = i * tm + lax.broadcasted_iota(jnp.int32, (tm, 1), 0)
    grp = lax.broadcasted_iota(jnp.int32, (tm, POOL_WIDTH), 1) // POOL_GROUP_DIM
    wsum = jnp.zeros((tm, POOL_WIDTH), F32)
    cnt = jnp.ones((tm, POOL_WIDTH), F32)
    for gi, win in enumerate(POOL_WINDOWS):
        lo = win // 2
        hi = win - lo - 1
        c = jnp.minimum(pos + hi + 1, seq_len) - jnp.maximum(pos - lo, 0)
        c = jnp.maximum(c, 1).astype(F32)
        sel = grp == gi
        wsum = jnp.where(sel, sums[gi][POOL_HALO:POOL_HALO + tm], wsum)
        cnt = jnp.where(sel, c, cnt)
    pooled = (wsum / cnt - cur).astype(BF16)
    y_pool = (_dot(pooled, pw_ref[...]) * ps_ref[...]).astype(BF16)

    og = of_ref[...] + ob_ref[...]
    og = og * lax.rsqrt(_group_mean_sq(og, bd_ref[...]) + RMS_EPS) * gn_ref[...]
    gate = gg_ref[...]
    y_gla = (og * (gate * jax.nn.sigmoid(gate))).astype(BF16)

    mixed = (_dot(oa_ref[...], wo_ref[0:ATT_WIDTH, :])
             + _dot(y_pool, wo_ref[ATT_WIDTH:ATT_WIDTH + POOL_WIDTH, :])
             + _dot(y_gla, wo_ref[ATT_WIDTH + POOL_WIDTH:MIX_WIDTH, :]))
    o_ref[...] = h_ref[...] + mixed


def _post(h, o_att, pu, o_f, o_b, gg, layer, seq_len, plan, prep):
    b, lp, _ = h.shape
    tm = plan["row_tile"]
    nt = lp // tm
    hb = tm // POOL_HALO
    row = lambda w: pl.BlockSpec((None, tm, w), lambda bb, i: (bb, i, 0))
    par = lambda *shape: pl.BlockSpec((None,) + shape, lambda bb, i: (layer,) + (0,) * len(shape))
    return pl.pallas_call(
        functools.partial(_post_kernel, seq_len, tm, nt),
        grid=(b, nt),
        in_specs=[
            row(D_MODEL),
            row(ATT_WIDTH),
            row(POOL_WIDTH),
            pl.BlockSpec((None, POOL_HALO, POOL_WIDTH), lambda bb, i: (bb, jnp.maximum(i * hb - 1, 0), 0)),
            pl.BlockSpec((None, POOL_HALO, POOL_WIDTH),
                         lambda bb, i: (bb, jnp.minimum((i + 1) * hb, lp // POOL_HALO - 1), 0)),
            row(GLA_WIDTH),
            row(GLA_WIDTH),
            row(GLA_WIDTH),
            par(POOL_WIDTH, POOL_WIDTH),
            par(1, POOL_WIDTH),
            par(1, GLA_WIDTH),
            pl.BlockSpec((V7X_MXU_DIM, V7X_MXU_DIM), lambda bb, i: (0, 0)),
            par(MIX_WIDTH, D_MODEL),
        ],
        out_specs=row(D_MODEL),
        out_shape=jax.ShapeDtypeStruct((b, lp, D_MODEL), F32),
        compiler_params=_params(("parallel", "parallel")),
        name="post",
    )(h, o_att, pu, pu, pu, o_f, o_b, gg, prep["pool_w"], prep["pool_scale"], prep["gla_norm"], prep["bd"],
      prep["w_out"])


def _rope_tables(lp):
    pos = jnp.arange(lp, dtype=F32)
    inv_freq = ROPE_THETA ** (-jnp.arange(0, ATT_HEAD_DIM, 2, dtype=F32) / ATT_HEAD_DIM)
    ang = pos[:, None] * inv_freq[None, :]
    cos, sin = jnp.cos(ang), jnp.sin(ang)
    return jnp.tile(cos, (1, 4)), jnp.tile(jnp.concatenate([-sin, sin], axis=1), (1, 2))


def _prepare(p, padded_lengths):
    eye = jnp.eye(V7X_MXU_DIM // ATT_HEAD_DIM, dtype=F32)
    bd = jnp.kron(eye, jnp.full((ATT_HEAD_DIM, ATT_HEAD_DIM), 1.0 / ATT_HEAD_DIM, F32)).astype(BF16)
    w_in = jnp.pad(p["w_in"], ((0, 0), (0, 0), (0, IN_COLS_PAD - IN_COLS))).astype(BF16)
    gu = jnp.zeros((DEPTH, V7X_LANES, 2 * GLA_WIDTH), F32)
    gu = gu.at[:, :GLA_GATE_RANK, :GLA_WIDTH].set(p["gla_gate_up"][:, 0])
    gu = gu.at[:, GLA_GATE_RANK:2 * GLA_GATE_RANK, GLA_WIDTH:].set(p["gla_gate_up"][:, 1])
    pool_w = jnp.zeros((DEPTH, POOL_WIDTH, POOL_WIDTH), F32)
    for gi in range(len(POOL_WINDOWS)):
        sl = slice(gi * POOL_GROUP_DIM, (gi + 1) * POOL_GROUP_DIM)
        pool_w = pool_w.at[:, sl, sl].set(p["pool_w"][:, gi])
    tables = {lp: _rope_tables(lp) for lp in padded_lengths}
    return dict(
        bd=bd,
        w_in=w_in,
        w_out=p["w_out"].astype(BF16),
        mix_norm=p["mix_norm"][:, None, :],
        q_gain=jnp.tile(p["attn_q_norm"], (1, 2))[:, None, :],
        k_gain=jnp.tile(p["attn_k_norm"], (1, 2))[:, None, :],
        lam=jnp.stack([p["lambda_q1"], p["lambda_k1"], p["lambda_q2"], p["lambda_k2"]], axis=1),
        sub_norm=p["attn_sub_norm"][:, None, :],
        score_bound=(1.02 * ATT_HEAD_DIM * ATT_HEAD_DIM ** -0.5 * LOG2E
                     * jnp.max(jnp.abs(p["attn_q_norm"]), axis=1) * jnp.max(jnp.abs(p["attn_k_norm"]), axis=1)),
        gate_up=gu,
        gate_bias=p["gla_gate_bias"].reshape(DEPTH, 1, 2 * GLA_WIDTH),
        pool_w=pool_w.astype(BF16),
        pool_scale=p["pool_scale"][:, None, :],
        gla_norm=jnp.tile(p["gla_out_norm"], (1, GLA_HEADS))[:, None, :],
        ffn1_norm=p["ffn1_norm"][:, None, :],
        ffn2_norm=p["ffn2_norm"][:, None, :],
        ffn1_w_in=p["ffn1_w_in"].astype(BF16),
        ffn1_w_out=p["ffn1_w_out"].astype(BF16),
        ffn2_w_in=p["ffn2_w_in"].astype(BF16),
        ffn2_w_out=p["ffn2_w_out"].astype(BF16),
        cos={lp: t[0] for lp, t in tables.items()},
        sin={lp: t[1] for lp, t in tables.items()},
    )


def _trunk(x, meta_tokens, prep):
    b, s, _ = x.shape
    seq_len = s + N_META
    plan = _plan(seq_len)
    lp = plan["lp"]
    meta = jnp.broadcast_to(meta_tokens.astype(x.dtype)[None], (b, N_META, D_MODEL))
    h = jnp.concatenate([meta, x, jnp.zeros((b, lp - seq_len, D_MODEL), x.dtype)], axis=1)
    for layer in range(DEPTH):
        lam_init = 0.8 - 0.6 * math.exp(-0.3 * layer)
        h = _ffn(h.reshape(b * lp, D_MODEL), layer, prep["ffn1_norm"], prep["ffn1_w_in"],
                 prep["ffn1_w_out"]).reshape(b, lp, D_MODEL)
        q, k, v, pu, gq, gk, gv, gg, gf, gb = _proj(h, layer, seq_len, plan, prep)
        o_att = _attn(q, k, v, layer, seq_len, plan, prep, lam_init)
        o_f = _gla(gq, gk, gv, gf, plan, reverse=False)
        o_b = _gla(gq, gk, gv, gb, plan, reverse=True)
        h = _post(h, o_att, pu, o_f, o_b, gg, layer, seq_len, plan, prep)
        h = _ffn(h.reshape(b * lp, D_MODEL), layer, prep["ffn2_norm"], prep["ffn2_w_in"],
                 prep["ffn2_w_out"]).reshape(b, lp, D_MODEL)
    return h[:, N_META:seq_len]


def kernel(x_prompt, x_sample, meta_tokens, ffn1_norm, ffn1_w_in, ffn1_w_out, mix_norm, w_in, w_out, attn_q_norm,
           attn_k_norm, lambda_q1, lambda_k1, lambda_q2, lambda_k2, attn_sub_norm, pool_w, pool_scale, gla_gate_up,
           gla_gate_bias, gla_out_norm, ffn2_norm, ffn2_w_in, ffn2_w_out):
    p = dict(ffn1_norm=ffn1_norm, ffn1_w_in=ffn1_w_in, ffn1_w_out=ffn1_w_out, mix_norm=mix_norm, w_in=w_in,
             w_out=w_out, attn_q_norm=attn_q_norm, attn_k_norm=attn_k_norm, lambda_q1=lambda_q1,
             lambda_k1=lambda_k1, lambda_q2=lambda_q2, lambda_k2=lambda_k2, attn_sub_norm=attn_sub_norm,
             pool_w=pool_w, pool_scale=pool_scale, gla_gate_up=gla_gate_up, gla_gate_bias=gla_gate_bias,
             gla_out_norm=gla_out_norm, ffn2_norm=ffn2_norm, ffn2_w_in=ffn2_w_in, ffn2_w_out=ffn2_w_out)
    lengths = {_plan(x.shape[1] + N_META)["lp"] for x in (x_prompt, x_sample)}
    prep = _prepare(p, lengths)
    return (_trunk(x_prompt, meta_tokens, prep), _trunk(x_sample, meta_tokens, prep))
```

```python
import functools
import math

import jax
import jax.numpy as jnp
from jax import lax
from jax.experimental import pallas as pl
from jax.experimental.pallas import tpu as pltpu

F32 = jnp.float32
BF16 = jnp.bfloat16

D_MODEL = 1024
D_FF = 2816
DEPTH = 4
N_META = 16
RMS_EPS = 1e-6
ROPE_THETA = 10000.0

ATT_HEADS = 4
ATT_HEAD_DIM = 64
ATT_V_DIM = 2 * ATT_HEAD_DIM
ATT_WIDTH = ATT_HEADS * ATT_V_DIM

POOL_WINDOWS = (2, 4, 8, 16)
POOL_GROUP_DIM = 64
POOL_WIDTH = len(POOL_WINDOWS) * POOL_GROUP_DIM
POOL_HALO = 8

GLA_HEADS = 4
GLA_HEAD_DIM = 64
GLA_WIDTH = GLA_HEADS * GLA_HEAD_DIM
GLA_GATE_RANK = 16
GLA_GATE_TEMP = 16.0
GLA_CHUNK = 64
GLA_SUB = 16
GLA_NSUB = GLA_CHUNK // GLA_SUB
GLA_EXP_CLAMP = 80.0

MIX_WIDTH = ATT_WIDTH + POOL_WIDTH + GLA_WIDTH
IN_COLS = 4 * ATT_WIDTH - ATT_WIDTH + POOL_WIDTH + 4 * GLA_WIDTH + 2 * GLA_GATE_RANK
GATE_COL0 = IN_COLS - 2 * GLA_GATE_RANK
IN_COLS_PAD = GATE_COL0 + 128

V7X_LANES = 128
V7X_MXU_DIM = 256
SEQ_ALIGN = 128
SHORT_SEQ_ROWS = 2304
ATT_Q_ROWS_CAP = 16640
ATT_KV_TILE_LONG = 1280
ATT_QUERY_BLOCK = V7X_MXU_DIM
ATT_KEY_BLOCK = V7X_MXU_DIM
ATT_SKEW_RUNNING_MAX = 5
ATT_SKEW_DIRECT = 2
ATT_DIRECT_BOUND = 90.0
ROW_TILE_CAP = 704
GLA_TILE_CAP = 1088
FFN_TILE_CAP = 704
VMEM_LIMIT_BYTES = 56 * 1024 * 1024
FF_TILE = 256
LOG2E = 1.4426950408889634


def _divisor_tile(n, cap, mult):
    best = None
    for t in range(mult, min(n, cap) + 1, mult):
        if n % t == 0:
            best = t
    assert best is not None, (n, cap, mult)
    return best


def _plan(seq_len):
    if seq_len <= SHORT_SEQ_ROWS:
        lp = -(-seq_len // SEQ_ALIGN) * SEQ_ALIGN
        tk = tq = sub = lp
    else:
        tk = sub = ATT_KV_TILE_LONG
        lp = -(-seq_len // tk) * tk
        tq = _divisor_tile(lp, ATT_Q_ROWS_CAP, sub)
    return dict(
        lp=lp,
        row_tile=_divisor_tile(lp, ROW_TILE_CAP, 16),
        tq=tq,
        tk=tk,
        sub=sub,
        gla_tile=_divisor_tile(lp, GLA_TILE_CAP, GLA_CHUNK),
    )


def _params(sem):
    return pltpu.CompilerParams(dimension_semantics=sem, vmem_limit_bytes=VMEM_LIMIT_BYTES)


def _dot(a, b):
    return jnp.dot(a, b, preferred_element_type=F32)


def _dot_nt(a, b):
    return lax.dot_general(a, b, (((1,), (1,)), ((), ())), preferred_element_type=F32)


def _dot_tn(a, b):
    return lax.dot_general(a, b, (((0,), (0,)), ((), ())), preferred_element_type=F32)


def _split2(x):
    hi = x.astype(BF16)
    lo = (x - hi.astype(F32)).astype(BF16)
    return hi, lo


def _group_mean_sq(x, bd):
    hi, lo = _split2(x * x)
    outs = []
    for c in range(x.shape[1] // V7X_MXU_DIM):
        sl = slice(V7X_MXU_DIM * c, V7X_MXU_DIM * (c + 1))
        outs.append(_dot(hi[:, sl], bd) + _dot(lo[:, sl], bd))
    return outs[0] if len(outs) == 1 else jnp.concatenate(outs, axis=1)


def _ffn_rows(x, gain, wi_ref, wo_ref):
    ms = jnp.mean(x * x, axis=-1, keepdims=True)
    xn = (x * lax.rsqrt(ms + RMS_EPS) * gain).astype(BF16)
    acc = None
    for c0 in range(0, D_FF, FF_TILE):
        g = _dot(xn, wi_ref[:, c0:c0 + FF_TILE])
        u = _dot(xn, wi_ref[:, D_FF + c0:D_FF + c0 + FF_TILE])
        a = (g * jax.nn.sigmoid(g) * u).astype(BF16)
        d = _dot(a, wo_ref[c0:c0 + FF_TILE, :])
        acc = d if acc is None else acc + d
    return x + 0.5 * acc


def _ffn_kernel(x_ref, g_ref, wi_ref, wo_ref, o_ref):
    o_ref[...] = _ffn_rows(x_ref[...], g_ref[...], wi_ref, wo_ref)


def _resident(layer, *shape):
    return pl.BlockSpec((None,) + shape, lambda *_: (layer,) + (0,) * len(shape), pipeline_mode=pl.Buffered(1))


def _ffn(h2d, layer, norm, w_in, w_out):
    rows = h2d.shape[0]
    tm = _divisor_tile(rows, FFN_TILE_CAP, 16)
    resident = functools.partial(_resident, layer)
    return pl.pallas_call(
        _ffn_kernel,
        grid=(rows // tm,),
        in_specs=[
            pl.BlockSpec((tm, D_MODEL), lambda i: (i, 0)),
            pl.BlockSpec((None, 1, D_MODEL), lambda i: (layer, 0, 0)),
            resident(D_MODEL, 2 * D_FF),
            resident(D_FF, D_MODEL),
        ],
        out_specs=pl.BlockSpec((tm, D_MODEL), lambda i: (i, 0)),
        out_shape=jax.ShapeDtypeStruct((rows, D_MODEL), F32),
        compiler_params=_params(("parallel",)),
        name="ffn",
    )(h2d, norm, w_in, w_out)


def _rope128(x, cos, sin_signed, first_half):
    partner = jnp.where(first_half, pltpu.roll(x, 96, 1), pltpu.roll(x, 32, 1))
    return x * cos + partner * sin_signed


def _proj_kernel(seq_len, tm, h_ref, g_ref, w_ref, cos_ref, sin_ref, qg_ref, kg_ref, bd_ref, gu_ref, gbias_ref,
                 q_ref, k_ref, v_ref, pu_ref, gq_ref, gk_ref, gv_ref, gg_ref, gf_ref, gb_ref):
    i = pl.program_id(1)
    x = h_ref[...]
    pos = i * tm + lax.broadcasted_iota(jnp.int32, (tm, 1), 0)
    valid = pos < seq_len
    ms = jnp.mean(x * x, axis=-1, keepdims=True)
    xn = jnp.where(valid, x * lax.rsqrt(ms + RMS_EPS) * g_ref[...], 0.0).astype(BF16)

    bd = bd_ref[...]
    cos = cos_ref[...]
    sin = sin_ref[...]
    lane = lax.broadcasted_iota(jnp.int32, (tm, V7X_LANES), 1)
    first_half = (lane % ATT_HEAD_DIM) < (ATT_HEAD_DIM // 2)

    def qk_path(col0, gain_ref, out_ref, out_scale):
        z = _dot(xn, w_ref[:, col0:col0 + ATT_WIDTH])
        zn = z * lax.rsqrt(_group_mean_sq(z, bd) + RMS_EPS)
        gain = gain_ref[...]
        for c in range(ATT_WIDTH // V7X_LANES):
            sl = slice(V7X_LANES * c, V7X_LANES * (c + 1))
            r = _rope128(zn[:, sl] * gain, cos, sin, first_half)
            out_ref[:, sl] = (r * out_scale).astype(BF16)

    qk_path(0, qg_ref, q_ref, ATT_HEAD_DIM ** -0.5 * LOG2E)
    qk_path(ATT_WIDTH, kg_ref, k_ref, 1.0)

    v_ref[...] = _dot(xn, w_ref[:, 2 * ATT_WIDTH:3 * ATT_WIDTH]).astype(BF16)

    c0 = 3 * ATT_WIDTH
    pu_ref[...] = _dot(xn, w_ref[:, c0:c0 + POOL_WIDTH])
    c0 += POOL_WIDTH
    gq_ref[...] = (_dot(xn, w_ref[:, c0:c0 + GLA_WIDTH]) * (GLA_HEAD_DIM ** -0.5)).astype(BF16)
    c0 += GLA_WIDTH
    gk_ref[...] = _dot(xn, w_ref[:, c0:c0 + GLA_WIDTH]).astype(BF16)
    c0 += GLA_WIDTH
    gv_ref[...] = _dot(xn, w_ref[:, c0:c0 + GLA_WIDTH]).astype(BF16)
    c0 += GLA_WIDTH
    gg_ref[...] = _dot(xn, w_ref[:, c0:c0 + GLA_WIDTH])

    glr = _dot(xn, w_ref[:, GATE_COL0:IN_COLS_PAD])
    glr_hi, glr_lo = _split2(glr)
    gu_hi, gu_lo = _split2(gu_ref[...])
    logit = _dot(glr_hi, gu_hi) + _dot(glr_lo, gu_hi) + _dot(glr_hi, gu_lo) + gbias_ref[...]
    log_sig = jnp.minimum(logit, 0.0) - jnp.log(1.0 + jnp.exp(-jnp.abs(logit)))
    log_a = log_sig * (1.0 / GLA_GATE_TEMP)
    gf_ref[...] = log_a[:, :GLA_WIDTH]
    gb_ref[...] = log_a[:, GLA_WIDTH:]


def _proj(h, layer, seq_len, plan, prep):
    b, lp, _ = h.shape
    tm = plan["row_tile"]
    row = lambda w: pl.BlockSpec((None, tm, w), lambda bb, i: (bb, i, 0))
    par = lambda *shape: pl.BlockSpec((None,) + shape, lambda bb, i: (layer,) + (0,) * len(shape))
    out_widths = [(ATT_WIDTH, BF16), (ATT_WIDTH, BF16), (ATT_WIDTH, BF16), (POOL_WIDTH, F32),
                  (GLA_WIDTH, BF16), (GLA_WIDTH, BF16), (GLA_WIDTH, BF16), (GLA_WIDTH, F32),
                  (GLA_WIDTH, F32), (GLA_WIDTH, F32)]
    return pl.pallas_call(
        functools.partial(_proj_kernel, seq_len, tm),
        grid=(b, lp // tm),
        in_specs=[
            row(D_MODEL),
            par(1, D_MODEL),
            par(D_MODEL, IN_COLS_PAD),
            pl.BlockSpec((tm, V7X_LANES), lambda bb, i: (i, 0)),
            pl.BlockSpec((tm, V7X_LANES), lambda bb, i: (i, 0)),
            par(1, V7X_LANES),
            par(1, V7X_LANES),
            pl.BlockSpec((V7X_MXU_DIM, V7X_MXU_DIM), lambda bb, i: (0, 0)),
            par(V7X_LANES, 2 * GLA_WIDTH),
            par(1, 2 * GLA_WIDTH),
        ],
        out_specs=[row(w) for w, _ in out_widths],
        out_shape=[jax.ShapeDtypeStruct((b, lp, w), dt) for w, dt in out_widths],
        compiler_params=_params(("parallel", "parallel")),
        name="proj",
    )(h, prep["mix_norm"], prep["w_in"], prep["cos"][lp], prep["sin"][lp], prep["q_gain"], prep["k_gain"],
      prep["bd"], prep["gate_up"], prep["gate_bias"])


def _attn_kernel(seq_len, sub, nsub, tk, qb, nk, layer, lam_init, bound_ref, q_ref, k_ref, v_ref, lam_ref, sn_ref,
                 o_ref, m_ref, l_ref, acc_ref):
    ik = pl.program_id(3)
    direct = bound_ref[layer] <= ATT_DIRECT_BOUND

    def for_each_sub_tile(body):
        if nsub == 1:
            body(0)
        else:
            lax.fori_loop(0, nsub, lambda t, carry: (body(t), carry)[1], 0)

    @pl.when(ik == 0)
    def _():
        m_ref[...] = jnp.full_like(m_ref, -jnp.inf)
        l_ref[...] = jnp.zeros_like(l_ref)
        acc_ref[...] = jnp.zeros_like(acc_ref)

    def step(mask_tail, running_max):
        k = k_ref[...]
        vt = v_ref[...].T
        lane = lax.broadcasted_iota(jnp.int32, k.shape, 1)
        halves = (jnp.where(lane < ATT_HEAD_DIM, k, jnp.zeros_like(k)),
                  jnp.where(lane >= ATT_HEAD_DIM, k, jnp.zeros_like(k)))
        if mask_tail:
            key_ok = (ik * tk + lax.broadcasted_iota(jnp.int32, (tk, 1), 0)) < seq_len
        kb = _divisor_tile(tk, ATT_KEY_BLOCK, V7X_MXU_DIM) if (running_max and tk % V7X_MXU_DIM == 0) else tk
        skew = ATT_SKEW_RUNNING_MAX if running_max else ATT_SKEW_DIRECT
        chains = [(slice(r0, r0 + kb), c0, min(qb, sub - c0), c)
                  for r0 in range(0, tk, kb) for c0 in range(0, sub, qb) for c in (0, 1)]

        def sub_tile(t):
            base = t * sub if nsub == 1 else pl.multiple_of(t * sub, sub)

            def scores(keys, c0, width, c):
                return _dot_nt(halves[c][keys, :], q_ref[pl.ds(base + c0, width), :])

            def softmax_update(keys, c0, width, c, st):
                cols = slice(c0, c0 + width)
                if mask_tail:
                    st = jnp.where(key_ok[keys, :], st, -jnp.inf)
                if not running_max:
                    pt = jnp.exp2(st)
                    l_ref[t, c, :, cols] += jnp.sum(pt, axis=0, keepdims=True)
                    acc_ref[t, c, :, cols] += _dot(vt[:, keys], pt.astype(BF16))
                    return
                m_prev = m_ref[t, c, :, cols]
                m_new = jnp.maximum(m_prev, jnp.max(st, axis=0, keepdims=True))
                alpha = jnp.exp2(m_prev - m_new)
                pt = jnp.exp2(st - m_new)
                l_ref[t, c, :, cols] = alpha * l_ref[t, c, :, cols] + jnp.sum(pt, axis=0, keepdims=True)
                acc_ref[t, c, :, cols] = alpha * acc_ref[t, c, :, cols] + _dot(vt[:, keys], pt.astype(BF16))
                m_ref[t, c, :, cols] = m_new

            pending = {}
            for i in range(len(chains) + skew):
                if i < len(chains):
                    pending[i] = scores(*chains[i])
                if i >= skew:
                    softmax_update(*chains[i - skew], pending.pop(i - skew))

        for_each_sub_tile(sub_tile)

    def step_either_form(mask_tail):
        pl.when(direct)(lambda: step(mask_tail, running_max=False))
        pl.when(jnp.logical_not(direct))(lambda: step(mask_tail, running_max=True))

    if nk * tk == seq_len:
        step_either_form(False)
    elif nk == 1:
        step_either_form(True)
    else:
        pl.when(ik < nk - 1)(lambda: step_either_form(False))
        pl.when(ik == nk - 1)(lambda: step_either_form(True))

    @pl.when(ik == nk - 1)
    def _():
        lp = lam_ref[...]
        a1 = jnp.sum(lp[0:1] * lp[1:2], axis=-1, keepdims=True)
        a2 = jnp.sum(lp[2:3] * lp[3:4], axis=-1, keepdims=True)
        lam = jnp.exp(a1) - jnp.exp(a2) + lam_init

        def finish(t):
            base = t * sub if nsub == 1 else pl.multiple_of(t * sub, sub)
            ot = acc_ref[t, 0] / l_ref[t, 0] - lam * (acc_ref[t, 1] / l_ref[t, 1])
            ms = jnp.mean(ot * ot, axis=0, keepdims=True)
            ot = ot * (lax.rsqrt(ms + RMS_EPS) * (1.0 - lam_init))
            o_ref[pl.ds(base, sub), :] = (ot.T * sn_ref[...]).astype(BF16)

        for_each_sub_tile(finish)


def _attn(q, k, v, layer, seq_len, plan, prep, lam_init):
    b, lp, _ = q.shape
    tq, tk, sub = plan["tq"], plan["tk"], plan["sub"]
    nk = lp // tk
    nsub = tq // sub
    return pl.pallas_call(
        functools.partial(_attn_kernel, seq_len, sub, nsub, tk, ATT_QUERY_BLOCK, nk, layer, lam_init),
        grid=(b, ATT_HEADS, lp // tq, nk),
        in_specs=[
            pl.BlockSpec(memory_space=pltpu.SMEM),
            pl.BlockSpec((None, tq, ATT_V_DIM), lambda bb, hh, iq, ik: (bb, iq, hh)),
            pl.BlockSpec((None, tk, ATT_V_DIM), lambda bb, hh, iq, ik: (bb, ik, hh)),
            pl.BlockSpec((None, tk, ATT_V_DIM), lambda bb, hh, iq, ik: (bb, ik, hh)),
            pl.BlockSpec((None, 4, ATT_HEAD_DIM), lambda bb, hh, iq, ik: (layer, 0, 0)),
            pl.BlockSpec((None, 1, ATT_V_DIM), lambda bb, hh, iq, ik: (layer, 0, 0)),
        ],
        out_specs=pl.BlockSpec((None, tq, ATT_V_DIM), lambda bb, hh, iq, ik: (bb, iq, hh)),
        out_shape=jax.ShapeDtypeStruct((b, lp, ATT_WIDTH), BF16),
        scratch_shapes=[pltpu.VMEM((nsub, 2, 1, sub), F32), pltpu.VMEM((nsub, 2, 1, sub), F32),
                        pltpu.VMEM((nsub, 2, ATT_V_DIM, sub), F32)],
        compiler_params=_params(("parallel", "parallel", "parallel", "arbitrary")),
        name="attn",
    )(prep["score_bound"], q, k, v, prep["lam"], prep["sub_norm"])


class _GlaScan:
    def __init__(self, reverse, nchunk, q_ref, k_ref, v_ref, g_ref, o_ref):
        self.reverse, self.nchunk = reverse, nchunk
        self.q_ref, self.k_ref, self.v_ref, self.g_ref, self.o_ref = q_ref, k_ref, v_ref, g_ref, o_ref
        c_rows = GLA_CHUNK
        ti = lax.broadcasted_iota(jnp.int32, (2 * c_rows, c_rows), 0)
        si = lax.broadcasted_iota(jnp.int32, (2 * c_rows, c_rows), 1)
        t_loc = ti % c_rows
        incl = self.before(si, t_loc)
        blk = jnp.logical_and(self.before(si // GLA_SUB, t_loc // GLA_SUB), (si // GLA_SUB) != (t_loc // GLA_SUB))
        self.cum_mat = jnp.where(ti < c_rows, jnp.where(incl, 1.0, 0.0), jnp.where(blk, 1.0, 0.0)).astype(BF16)
        hr = lax.broadcasted_iota(jnp.int32, (GLA_WIDTH, GLA_WIDTH), 0) // GLA_HEAD_DIM
        hc = lax.broadcasted_iota(jnp.int32, (GLA_WIDTH, GLA_WIDTH), 1) // GLA_HEAD_DIM
        self.head_mask = hr == hc
        tt = lax.broadcasted_iota(jnp.int32, (c_rows, GLA_WIDTH), 0)
        ss = lax.broadcasted_iota(jnp.int32, (c_rows, GLA_WIDTH), 1) % c_rows
        self.causal = self.before(ss, tt)
        self.row_blk = lax.broadcasted_iota(jnp.int32, (c_rows, 1), 0) // GLA_SUB

    def before(self, a, b_):
        return (a >= b_) if self.reverse else (a <= b_)

    def rows_of(self, ci):
        c = (self.nchunk - 1 - ci) if self.reverse else ci
        return slice(c * GLA_CHUNK, (c + 1) * GLA_CHUNK)

    def stage_cumsum(self, ci):
        g_hi, g_lo = _split2(self.g_ref[self.rows_of(ci), :])
        return _dot(self.cum_mat, g_hi) + _dot(self.cum_mat, g_lo)

    def stage_intra(self, ci, cums):
        rows = self.rows_of(ci)
        q = self.q_ref[rows, :].astype(F32)
        k = self.k_ref[rows, :].astype(F32)
        b_incl = cums[:GLA_CHUNK]
        b_ref = cums[GLA_CHUNK:]
        b_last = b_incl[0:1] if self.reverse else b_incl[GLA_CHUNK - 1:GLA_CHUNK]
        qs = q * jnp.exp(b_incl - b_ref)
        qe = (q * jnp.exp(b_incl)).astype(BF16)
        kd = (k * jnp.exp(b_last - b_incl)).astype(BF16)
        lhs_parts, rhs_parts = [], []
        for i in range(GLA_NSUB):
            r_i = b_ref[GLA_SUB * i:GLA_SUB * i + 1]
            k_i = jnp.where(self.before(self.row_blk, i),
                            k * jnp.exp(jnp.minimum(r_i - b_incl, GLA_EXP_CLAMP)), 0.0)
            k_i4 = jnp.concatenate([k_i] * GLA_HEADS, axis=0)
            rhs_parts.append(jnp.where(self.head_mask, k_i4, 0.0).astype(BF16))
            lhs_parts.append(jnp.where(self.row_blk == i, qs, 0.0).astype(BF16))
        lhs = jnp.concatenate(lhs_parts, axis=1)
        rhs = jnp.concatenate(rhs_parts, axis=1)
        a = jnp.where(self.causal, _dot_nt(lhs, rhs), 0.0).astype(BF16)
        return a, qe, kd, jnp.exp(b_last)

    def stage_state(self, ci, st, a, qe, kd, decay):
        rows = self.rows_of(ci)
        v = self.v_ref[rows, :]
        v4 = jnp.where(self.head_mask, jnp.concatenate([v] * GLA_HEADS, axis=0), jnp.zeros((), BF16))
        self.o_ref[rows, :] = _dot(a, v4) + _dot_nt(qe, st.astype(BF16))
        return st * decay + jnp.where(self.head_mask, _dot_tn(v, kd), 0.0)


def _gla_kernel(nchunk, qf_ref, kf_ref, vf_ref, gf_ref, qb_ref, kb_ref, vb_ref, gb_ref, of_ref, ob_ref,
                stf_ref, stb_ref):
    @pl.when(pl.program_id(1) == 0)
    def _():
        stf_ref[...] = jnp.zeros_like(stf_ref)
        stb_ref[...] = jnp.zeros_like(stb_ref)

    scans = (_GlaScan(False, nchunk, qf_ref, kf_ref, vf_ref, gf_ref, of_ref),
             _GlaScan(True, nchunk, qb_ref, kb_ref, vb_ref, gb_ref, ob_ref))
    states = [stf_ref[...], stb_ref[...]]
    cums, intra = [{}, {}], [{}, {}]
    for step in range(nchunk + 2):
        for d, scan in enumerate(scans):
            if step < nchunk:
                cums[d][step] = scan.stage_cumsum(step)
        for d, scan in enumerate(scans):
            if 0 <= step - 1 < nchunk:
                intra[d][step - 1] = scan.stage_intra(step - 1, cums[d].pop(step - 1))
        for d, scan in enumerate(scans):
            if 0 <= step - 2 < nchunk:
                states[d] = scan.stage_state(step - 2, states[d], *intra[d].pop(step - 2))
    stf_ref[...] = states[0]
    stb_ref[...] = states[1]


def _gla(gq, gk, gv, g_fwd, g_bwd, plan):
    b, lp, _ = gq.shape
    tt = plan["gla_tile"]
    nt = lp // tt
    fwd = pl.BlockSpec((None, tt, GLA_WIDTH), lambda bb, i: (bb, i, 0))
    bwd = pl.BlockSpec((None, tt, GLA_WIDTH), lambda bb, i: (bb, nt - 1 - i, 0))
    out = jax.ShapeDtypeStruct((b, lp, GLA_WIDTH), F32)
    return pl.pallas_call(
        functools.partial(_gla_kernel, tt // GLA_CHUNK),
        grid=(b, nt),
        in_specs=[fwd, fwd, fwd, fwd, bwd, bwd, bwd, bwd],
        out_specs=[fwd, bwd],
        out_shape=[out, out],
        scratch_shapes=[pltpu.VMEM((GLA_WIDTH, GLA_WIDTH), F32), pltpu.VMEM((GLA_WIDTH, GLA_WIDTH), F32)],
        compiler_params=_params(("parallel", "arbitrary")),
        name="gla",
    )(gq, gk, gv, g_fwd, gq, gk, gv, g_bwd)


def _post_kernel(seq_len, tm, nt, h_ref, oa_ref, pu_ref, pup_ref, pun_ref, of_ref, ob_ref, gg_ref,
                 pw_ref, ps_ref, gn_ref, bd_ref, wo_ref, o_ref):
    i = pl.program_id(1)
    cur = pu_ref[...]
    prev = jnp.where(i > 0, pup_ref[...], 0.0)
    nxt = jnp.where(i < nt - 1, pun_ref[...], 0.0)
    ext = jnp.concatenate([prev, cur, nxt], axis=0)
    n = tm + 2 * POOL_HALO
    p2 = ext + pltpu.roll(ext, 1, 0)
    p4 = p2 + pltpu.roll(p2, 2, 0)
    p8 = p4 + pltpu.roll(p4, 4, 0)
    p16 = p8 + pltpu.roll(p8, 8, 0)
    sums = (p2, pltpu.roll(p4, n - 1, 0), pltpu.roll(p8, n - 3, 0), pltpu.roll(p16, n - 7, 0))

    pos = i * tm + lax.broadcasted_iota(jnp.int32, (tm, 1), 0)
    grp = lax.broadcasted_iota(jnp.int32, (tm, POOL_WIDTH), 1) // POOL_GROUP_DIM
    wsum = jnp.zeros((tm, POOL_WIDTH), F32)
    cnt = jnp.ones((tm, POOL_WIDTH), F32)
    for gi, win in enumerate(POOL_WINDOWS):
        lo = win // 2
        hi = win - lo - 1
        c = jnp.minimum(pos + hi + 1, seq_len) - jnp.maximum(pos - lo, 0)
        c = jnp.maximum(c, 1).astype(F32)
        sel = grp == gi
        wsum = jnp.where(sel, sums[gi][POOL_HALO:POOL_HALO + tm], wsum)
        cnt = jnp.where(sel, c, cnt)
    pooled = (wsum / cnt - cur).astype(BF16)
    y_pool = (_dot(pooled, pw_ref[...]) * ps_ref[...]).astype(BF16)

    og = of_ref[...] + ob_ref[...]
    og = og * lax.rsqrt(_group_mean_sq(og, bd_ref[...]) + RMS_EPS) * gn_ref[...]
    gate = gg_ref[...]
    y_gla = (og * (gate * jax.nn.sigmoid(gate))).astype(BF16)

    mixed = (_dot(oa_ref[...], wo_ref[0:ATT_WIDTH, :])
             + _dot(y_pool, wo_ref[ATT_WIDTH:ATT_WIDTH + POOL_WIDTH, :])
             + _dot(y_gla, wo_ref[ATT_WIDTH + POOL_WIDTH:MIX_WIDTH, :]))
    o_ref[...] = h_ref[...] + mixed


def _post(h, o_att, pu, o_f, o_b, gg, layer, seq_len, plan, prep):
    b, lp, _ = h.shape
    tm = plan["row_tile"]
    nt = lp // tm
    hb = tm // POOL_HALO
    row = lambda w: pl.BlockSpec((None, tm, w), lambda bb, i: (bb, i, 0))
    par = lambda *shape: pl.BlockSpec((None,) + shape, lambda bb, i: (layer,) + (0,) * len(shape))
    return pl.pallas_call(
        functools.partial(_post_kernel, seq_len, tm, nt),
        grid=(b, nt),
        in_specs=[
            row(D_MODEL),
            row(ATT_WIDTH),
            row(POOL_WIDTH),
            pl.BlockSpec((None, POOL_HALO, POOL_WIDTH), lambda bb, i: (bb, jnp.maximum(i * hb - 1, 0), 0)),
            pl.BlockSpec((None, POOL_HALO, POOL_WIDTH),
                         lambda bb, i: (bb, jnp.minimum((i + 1) * hb, lp // POOL_HALO - 1), 0)),
            row(GLA_WIDTH),
            row(GLA_WIDTH),
            row(GLA_WIDTH),
            par(POOL_WIDTH, POOL_WIDTH),
            par(1, POOL_WIDTH),
            par(1, GLA_WIDTH),
            pl.BlockSpec((V7X_MXU_DIM, V7X_MXU_DIM), lambda bb, i: (0, 0)),
            par(MIX_WIDTH, D_MODEL),
        ],
        out_specs=row(D_MODEL),
        out_shape=jax.ShapeDtypeStruct((b, lp, D_MODEL), F32),
        compiler_params=_params(("parallel", "parallel")),
        name="post",
    )(h, o_att, pu, pu, pu, o_f, o_b, gg, prep["pool_w"], prep["pool_scale"], prep["gla_norm"], prep["bd"],
      prep["w_out"])


def _rope_tables(lp):
    pos = jnp.arange(lp, dtype=F32)
    inv_freq = ROPE_THETA ** (-jnp.arange(0, ATT_HEAD_DIM, 2, dtype=F32) / ATT_HEAD_DIM)
    ang = pos[:, None] * inv_freq[None, :]
    cos, sin = jnp.cos(ang), jnp.sin(ang)
    return jnp.tile(cos, (1, 4)), jnp.tile(jnp.concatenate([-sin, sin], axis=1), (1, 2))


def _prepare(p, padded_lengths):
    eye = jnp.eye(V7X_MXU_DIM // ATT_HEAD_DIM, dtype=F32)
    bd = jnp.kron(eye, jnp.full((ATT_HEAD_DIM, ATT_HEAD_DIM), 1.0 / ATT_HEAD_DIM, F32)).astype(BF16)
    w_in = jnp.pad(p["w_in"], ((0, 0), (0, 0), (0, IN_COLS_PAD - IN_COLS))).astype(BF16)
    gu = jnp.zeros((DEPTH, V7X_LANES, 2 * GLA_WIDTH), F32)
    gu = gu.at[:, :GLA_GATE_RANK, :GLA_WIDTH].set(p["gla_gate_up"][:, 0])
    gu = gu.at[:, GLA_GATE_RANK:2 * GLA_GATE_RANK, GLA_WIDTH:].set(p["gla_gate_up"][:, 1])
    pool_w = jnp.zeros((DEPTH, POOL_WIDTH, POOL_WIDTH), F32)
    for gi in range(len(POOL_WINDOWS)):
        sl = slice(gi * POOL_GROUP_DIM, (gi + 1) * POOL_GROUP_DIM)
        pool_w = pool_w.at[:, sl, sl].set(p["pool_w"][:, gi])
    tables = {lp: _rope_tables(lp) for lp in padded_lengths}
    return dict(
        bd=bd,
        w_in=w_in,
        w_out=p["w_out"].astype(BF16),
        mix_norm=p["mix_norm"][:, None, :],
        q_gain=jnp.tile(p["attn_q_norm"], (1, 2))[:, None, :],
        k_gain=jnp.tile(p["attn_k_norm"], (1, 2))[:, None, :],
        lam=jnp.stack([p["lambda_q1"], p["lambda_k1"], p["lambda_q2"], p["lambda_k2"]], axis=1),
        sub_norm=p["attn_sub_norm"][:, None, :],
        score_bound=(1.02 * ATT_HEAD_DIM * ATT_HEAD_DIM ** -0.5 * LOG2E
                     * jnp.max(jnp.abs(p["attn_q_norm"]), axis=1) * jnp.max(jnp.abs(p["attn_k_norm"]), axis=1)),
        gate_up=gu,
        gate_bias=p["gla_gate_bias"].reshape(DEPTH, 1, 2 * GLA_WIDTH),
        pool_w=pool_w.astype(BF16),
        pool_scale=p["pool_scale"][:, None, :],
        gla_norm=jnp.tile(p["gla_out_norm"], (1, GLA_HEADS))[:, None, :],
        ffn1_norm=p["ffn1_norm"][:, None, :],
        ffn2_norm=p["ffn2_norm"][:, None, :],
        ffn1_w_in=p["ffn1_w_in"].astype(BF16),
        ffn1_w_out=p["ffn1_w_out"].astype(BF16),
        ffn2_w_in=p["ffn2_w_in"].astype(BF16),
        ffn2_w_out=p["ffn2_w_out"].astype(BF16),
        cos={lp: t[0] for lp, t in tables.items()},
        sin={lp: t[1] for lp, t in tables.items()},
    )


def _trunk(x, meta_tokens, prep):
    b, s, _ = x.shape
    seq_len = s + N_META
    plan = _plan(seq_len)
    lp = plan["lp"]
    meta = jnp.broadcast_to(meta_tokens.astype(x.dtype)[None], (b, N_META, D_MODEL))
    h = jnp.concatenate([meta, x, jnp.zeros((b, lp - seq_len, D_MODEL), x.dtype)], axis=1)
    for layer in range(DEPTH):
        lam_init = 0.8 - 0.6 * math.exp(-0.3 * layer)
        h = _ffn(h.reshape(b * lp, D_MODEL), layer, prep["ffn1_norm"], prep["ffn1_w_in"],
                 prep["ffn1_w_out"]).reshape(b, lp, D_MODEL)
        q, k, v, pu, gq, gk, gv, gg, gf, gb = _proj(h, layer, seq_len, plan, prep)
        o_att = _attn(q, k, v, layer, seq_len, plan, prep, lam_init)
        o_f, o_b = _gla(gq, gk, gv, gf, gb, plan)
        h = _post(h, o_att, pu, o_f, o_b, gg, layer, seq_len, plan, prep)
        h = _ffn(h.reshape(b * lp, D_MODEL), layer, prep["ffn2_norm"], prep["ffn2_w_in"],
                 prep["ffn2_w_out"]).reshape(b, lp, D_MODEL)
    return h[:, N_META:seq_len]


def kernel(x_prompt, x_sample, meta_tokens, ffn1_norm, ffn1_w_in, ffn1_w_out, mix_norm, w_in, w_out, attn_q_norm,
           attn_k_norm, lambda_q1, lambda_k1, lambda_q2, lambda_k2, attn_sub_norm, pool_w, pool_scale, gla_gate_up,
           gla_gate_bias, gla_out_norm, ffn2_norm, ffn2_w_in, ffn2_w_out):
    p = dict(ffn1_norm=ffn1_norm, ffn1_w_in=ffn1_w_in, ffn1_w_out=ffn1_w_out, mix_norm=mix_norm, w_in=w_in,
             w_out=w_out, attn_q_norm=attn_q_norm, attn_k_norm=attn_k_norm, lambda_q1=lambda_q1,
             lambda_k1=lambda_k1, lambda_q2=lambda_q2, lambda_k2=lambda_k2, attn_sub_norm=attn_sub_norm,
             pool_w=pool_w, pool_scale=pool_scale, gla_gate_up=gla_gate_up, gla_gate_bias=gla_gate_bias,
             gla_out_norm=gla_out_norm, ffn2_norm=ffn2_norm, ffn2_w_in=ffn2_w_in, ffn2_w_out=ffn2_w_out)
    lengths = {_plan(x.shape[1] + N_META)["lp"] for x in (x_prompt, x_sample)}
    prep = _prepare(p, lengths)
    return (_trunk(x_prompt, meta_tokens, prep), _trunk(x_sample, meta_tokens, prep))
```

```python
import functools
import math

import jax
import jax.numpy as jnp
from jax import lax
from jax.experimental import pallas as pl
from jax.experimental.pallas import tpu as pltpu

F32 = jnp.float32
BF16 = jnp.bfloat16

D_MODEL = 1024
D_FF = 2816
DEPTH = 4
N_META = 16
RMS_EPS = 1e-6
ROPE_THETA = 10000.0

ATT_HEADS = 4
ATT_HEAD_DIM = 64
ATT_V_DIM = 2 * ATT_HEAD_DIM
ATT_WIDTH = ATT_HEADS * ATT_V_DIM

POOL_WINDOWS = (2, 4, 8, 16)
POOL_GROUP_DIM = 64
POOL_WIDTH = len(POOL_WINDOWS) * POOL_GROUP_DIM
POOL_HALO = 8

GLA_HEADS = 4
GLA_HEAD_DIM = 64
GLA_WIDTH = GLA_HEADS * GLA_HEAD_DIM
GLA_GATE_RANK = 16
GLA_GATE_TEMP = 16.0
GLA_CHUNK = 64
GLA_SUB = 16
GLA_NSUB = GLA_CHUNK // GLA_SUB
GLA_EXP_CLAMP = 80.0

MIX_WIDTH = ATT_WIDTH + POOL_WIDTH + GLA_WIDTH
IN_COLS = 4 * ATT_WIDTH - ATT_WIDTH + POOL_WIDTH + 4 * GLA_WIDTH + 2 * GLA_GATE_RANK
GATE_COL0 = IN_COLS - 2 * GLA_GATE_RANK
IN_COLS_PAD = GATE_COL0 + 128

V7X_LANES = 128
V7X_MXU_DIM = 256
SEQ_ALIGN = 128
SHORT_SEQ_ROWS = 2304
ATT_Q_ROWS_CAP = 16640
ATT_KV_TILE_LONG = 1280
ATT_QUERY_BLOCK = V7X_MXU_DIM
ATT_KEY_BLOCK = V7X_MXU_DIM
ATT_SKEW_RUNNING_MAX = 5
ATT_SKEW_DIRECT = 2
ATT_DIRECT_BOUND = 90.0
ROW_TILE_CAP = 704
GLA_TILE_CAP = 1088
FFN_TILE_CAP = 704
VMEM_LIMIT_BYTES = 56 * 1024 * 1024
FF_TILE = 256
LOG2E = 1.4426950408889634


def _divisor_tile(n, cap, mult):
    best = None
    for t in range(mult, min(n, cap) + 1, mult):
        if n % t == 0:
            best = t
    assert best is not None, (n, cap, mult)
    return best


def _plan(seq_len):
    if seq_len <= SHORT_SEQ_ROWS:
        lp = -(-seq_len // SEQ_ALIGN) * SEQ_ALIGN
        tk = tq = sub = lp
    else:
        tk = sub = ATT_KV_TILE_LONG
        lp = -(-seq_len // tk) * tk
        tq = _divisor_tile(lp, ATT_Q_ROWS_CAP, sub)
    return dict(
        lp=lp,
        row_tile=_divisor_tile(lp, ROW_TILE_CAP, 16),
        tq=tq,
        tk=tk,
        sub=sub,
        gla_tile=_divisor_tile(lp, GLA_TILE_CAP, GLA_CHUNK),
    )


def _params(sem):
    return pltpu.CompilerParams(dimension_semantics=sem, vmem_limit_bytes=VMEM_LIMIT_BYTES)


def _dot(a, b):
    return jnp.dot(a, b, preferred_element_type=F32)


def _dot_nt(a, b):
    return lax.dot_general(a, b, (((1,), (1,)), ((), ())), preferred_element_type=F32)


def _dot_tn(a, b):
    return lax.dot_general(a, b, (((0,), (0,)), ((), ())), preferred_element_type=F32)


def _split2(x):
    hi = x.astype(BF16)
    lo = (x - hi.astype(F32)).astype(BF16)
    return hi, lo


def _group_mean_sq(x, bd):
    sq = (x * x).astype(BF16)
    outs = []
    for c in range(x.shape[1] // V7X_MXU_DIM):
        sl = slice(V7X_MXU_DIM * c, V7X_MXU_DIM * (c + 1))
        outs.append(_dot(sq[:, sl], bd))
    return outs[0] if len(outs) == 1 else jnp.concatenate(outs, axis=1)


def _ffn_rows(x, gain, wi_ref, wo_ref):
    ms = jnp.mean(x * x, axis=-1, keepdims=True)
    xn = (x * lax.rsqrt(ms + RMS_EPS) * gain).astype(BF16)
    acc = None
    for c0 in range(0, D_FF, FF_TILE):
        g = _dot(xn, wi_ref[:, c0:c0 + FF_TILE])
        u = _dot(xn, wi_ref[:, D_FF + c0:D_FF + c0 + FF_TILE])
        a = (g * jax.nn.sigmoid(g) * u).astype(BF16)
        d = _dot(a, wo_ref[c0:c0 + FF_TILE, :])
        acc = d if acc is None else acc + d
    return x + 0.5 * acc


def _ffn_kernel(x_ref, g_ref, wi_ref, wo_ref, o_ref):
    o_ref[...] = _ffn_rows(x_ref[...], g_ref[...], wi_ref, wo_ref)


def _ffn_embed_kernel(n_tokens, tm, xa_ref, xb_ref, meta_ref, g_ref, wi_ref, wo_ref, o_ref):
    i = pl.program_id(1)
    x = jnp.concatenate([xa_ref[tm - N_META:, :], xb_ref[:tm - N_META, :]], axis=0)
    pos = i * tm + lax.broadcasted_iota(jnp.int32, (tm, 1), 0)
    x = jnp.where(jnp.logical_and(pos >= N_META, pos < N_META + n_tokens), x, 0.0)
    meta_rows = jnp.concatenate([meta_ref[...], jnp.zeros((tm - N_META, D_MODEL), F32)], axis=0)
    x = jnp.where(pos < N_META, meta_rows, x)
    o_ref[...] = _ffn_rows(x, g_ref[...], wi_ref, wo_ref)


def _ffn_unpad_kernel(ha_ref, hb_ref, g_ref, wi_ref, wo_ref, o_ref):
    x = jnp.concatenate([ha_ref[N_META:, :], hb_ref[:N_META, :]], axis=0)
    o_ref[...] = _ffn_rows(x, g_ref[...], wi_ref, wo_ref)


def _resident(layer, *shape):
    return pl.BlockSpec((None,) + shape, lambda *_: (layer,) + (0,) * len(shape), pipeline_mode=pl.Buffered(1))


def _ffn(h2d, layer, norm, w_in, w_out):
    rows = h2d.shape[0]
    tm = _divisor_tile(rows, FFN_TILE_CAP, 16)
    resident = functools.partial(_resident, layer)
    return pl.pallas_call(
        _ffn_kernel,
        grid=(rows // tm,),
        in_specs=[
            pl.BlockSpec((tm, D_MODEL), lambda i: (i, 0)),
            pl.BlockSpec((None, 1, D_MODEL), lambda i: (layer, 0, 0)),
            resident(D_MODEL, 2 * D_FF),
            resident(D_FF, D_MODEL),
        ],
        out_specs=pl.BlockSpec((tm, D_MODEL), lambda i: (i, 0)),
        out_shape=jax.ShapeDtypeStruct((rows, D_MODEL), F32),
        compiler_params=_params(("parallel",)),
        name="ffn",
    )(h2d, norm, w_in, w_out)


def _ffn_embed(x, meta, lp, layer, norm, w_in, w_out):
    b, n_tokens, _ = x.shape
    tm = _divisor_tile(lp, FFN_TILE_CAP, 16)
    last = -(-n_tokens // tm) - 1
    resident = functools.partial(_resident, layer)
    return pl.pallas_call(
        functools.partial(_ffn_embed_kernel, n_tokens, tm),
        grid=(b, lp // tm),
        in_specs=[
            pl.BlockSpec((None, tm, D_MODEL), lambda bb, i: (bb, jnp.clip(i - 1, 0, last), 0)),
            pl.BlockSpec((None, tm, D_MODEL), lambda bb, i: (bb, jnp.minimum(i, last), 0)),
            pl.BlockSpec((N_META, D_MODEL), lambda bb, i: (0, 0)),
            pl.BlockSpec((None, 1, D_MODEL), lambda bb, i: (layer, 0, 0)),
            resident(D_MODEL, 2 * D_FF),
            resident(D_FF, D_MODEL),
        ],
        out_specs=pl.BlockSpec((None, tm, D_MODEL), lambda bb, i: (bb, i, 0)),
        out_shape=jax.ShapeDtypeStruct((b, lp, D_MODEL), F32),
        compiler_params=_params(("parallel", "parallel")),
        name="ffn_embed",
    )(x, x, meta, norm, w_in, w_out)


def _ffn_unpad(h, n_tokens, layer, norm, w_in, w_out):
    b = h.shape[0]
    tm = _divisor_tile(n_tokens, FFN_TILE_CAP, 16)
    resident = functools.partial(_resident, layer)
    return pl.pallas_call(
        _ffn_unpad_kernel,
        grid=(b, n_tokens // tm),
        in_specs=[
            pl.BlockSpec((None, tm, D_MODEL), lambda bb, j: (bb, j, 0)),
            pl.BlockSpec((None, tm, D_MODEL), lambda bb, j: (bb, j + 1, 0)),
            pl.BlockSpec((None, 1, D_MODEL), lambda bb, j: (layer, 0, 0)),
            resident(D_MODEL, 2 * D_FF),
            resident(D_FF, D_MODEL),
        ],
        out_specs=pl.BlockSpec((None, tm, D_MODEL), lambda bb, j: (bb, j, 0)),
        out_shape=jax.ShapeDtypeStruct((b, n_tokens, D_MODEL), F32),
        compiler_params=_params(("parallel", "parallel")),
        name="ffn_unpad",
    )(h, h, norm, w_in, w_out)


def _rope128(x, cos, sin_signed, first_half):
    partner = jnp.where(first_half, pltpu.roll(x, 96, 1), pltpu.roll(x, 32, 1))
    return x * cos + partner * sin_signed


def _proj_kernel(seq_len, tm, h_ref, g_ref, w_ref, cos_ref, sin_ref, qg_ref, kg_ref, bd_ref, gu_ref, gbias_ref,
                 q_ref, k_ref, v_ref, pu_ref, gq_ref, gk_ref, gv_ref, gg_ref, gf_ref, gb_ref):
    i = pl.program_id(1)
    x = h_ref[...]
    pos = i * tm + lax.broadcasted_iota(jnp.int32, (tm, 1), 0)
    valid = pos < seq_len
    ms = jnp.mean(x * x, axis=-1, keepdims=True)
    xn = jnp.where(valid, x * lax.rsqrt(ms + RMS_EPS) * g_ref[...], 0.0).astype(BF16)

    bd = bd_ref[...]
    cos = cos_ref[...]
    sin = sin_ref[...]
    lane = lax.broadcasted_iota(jnp.int32, (tm, V7X_LANES), 1)
    first_half = (lane % ATT_HEAD_DIM) < (ATT_HEAD_DIM // 2)

    def qk_path(col0, gain_ref, out_ref, out_scale):
        z = _dot(xn, w_ref[:, col0:col0 + ATT_WIDTH])
        zn = z * lax.rsqrt(_group_mean_sq(z, bd) + RMS_EPS)
        gain = gain_ref[...]
        for c in range(ATT_WIDTH // V7X_LANES):
            sl = slice(V7X_LANES * c, V7X_LANES * (c + 1))
            r = _rope128(zn[:, sl] * gain, cos, sin, first_half)
            out_ref[:, sl] = (r * out_scale).astype(BF16)

    qk_path(0, qg_ref, q_ref, ATT_HEAD_DIM ** -0.5 * LOG2E)
    qk_path(ATT_WIDTH, kg_ref, k_ref, 1.0)

    v_ref[...] = _dot(xn, w_ref[:, 2 * ATT_WIDTH:3 * ATT_WIDTH]).astype(BF16)

    c0 = 3 * ATT_WIDTH
    pu_ref[...] = _dot(xn, w_ref[:, c0:c0 + POOL_WIDTH])
    c0 += POOL_WIDTH
    gq_ref[...] = (_dot(xn, w_ref[:, c0:c0 + GLA_WIDTH]) * (GLA_HEAD_DIM ** -0.5)).astype(BF16)
    c0 += GLA_WIDTH
    gk_ref[...] = _dot(xn, w_ref[:, c0:c0 + GLA_WIDTH]).astype(BF16)
    c0 += GLA_WIDTH
    gv_ref[...] = _dot(xn, w_ref[:, c0:c0 + GLA_WIDTH]).astype(BF16)
    c0 += GLA_WIDTH
    gg_ref[...] = _dot(xn, w_ref[:, c0:c0 + GLA_WIDTH])

    glr = _dot(xn, w_ref[:, GATE_COL0:IN_COLS_PAD])
    glr_hi, glr_lo = _split2(glr)
    gu_hi, gu_lo = _split2(gu_ref[...])
    logit = _dot(glr_hi, gu_hi) + _dot(glr_lo, gu_hi) + _dot(glr_hi, gu_lo) + gbias_ref[...]
    log_sig = jnp.minimum(logit, 0.0) - jnp.log(1.0 + jnp.exp(-jnp.abs(logit)))
    log_a = log_sig * (1.0 / GLA_GATE_TEMP)
    gf_ref[...] = log_a[:, :GLA_WIDTH]
    gb_ref[...] = log_a[:, GLA_WIDTH:]


def _proj(h, layer, seq_len, plan, prep):
    b, lp, _ = h.shape
    tm = plan["row_tile"]
    row = lambda w: pl.BlockSpec((None, tm, w), lambda bb, i: (bb, i, 0))
    par = lambda *shape: pl.BlockSpec((None,) + shape, lambda bb, i: (layer,) + (0,) * len(shape))
    out_widths = [(ATT_WIDTH, BF16), (ATT_WIDTH, BF16), (ATT_WIDTH, BF16), (POOL_WIDTH, F32),
                  (GLA_WIDTH, BF16), (GLA_WIDTH, BF16), (GLA_WIDTH, BF16), (GLA_WIDTH, F32),
                  (GLA_WIDTH, F32), (GLA_WIDTH, F32)]
    return pl.pallas_call(
        functools.partial(_proj_kernel, seq_len, tm),
        grid=(b, lp // tm),
        in_specs=[
            row(D_MODEL),
            par(1, D_MODEL),
            par(D_MODEL, IN_COLS_PAD),
            pl.BlockSpec((tm, V7X_LANES), lambda bb, i: (i, 0)),
            pl.BlockSpec((tm, V7X_LANES), lambda bb, i: (i, 0)),
            par(1, V7X_LANES),
            par(1, V7X_LANES),
            pl.BlockSpec((V7X_MXU_DIM, V7X_MXU_DIM), lambda bb, i: (0, 0)),
            par(V7X_LANES, 2 * GLA_WIDTH),
            par(1, 2 * GLA_WIDTH),
        ],
        out_specs=[row(w) for w, _ in out_widths],
        out_shape=[jax.ShapeDtypeStruct((b, lp, w), dt) for w, dt in out_widths],
        compiler_params=_params(("parallel", "parallel")),
        name="proj",
    )(h, prep["mix_norm"], prep["w_in"], prep["cos"][lp], prep["sin"][lp], prep["q_gain"], prep["k_gain"],
      prep["bd"], prep["gate_up"], prep["gate_bias"])


def _attn_kernel(seq_len, sub, nsub, tk, qb, nk, layer, lam_init, bound_ref, q_ref, k_ref, v_ref, lam_ref, sn_ref,
                 o_ref, m_ref, l_ref, acc_ref):
    ik = pl.program_id(3)
    direct = bound_ref[layer] <= ATT_DIRECT_BOUND

    def for_each_sub_tile(body):
        if nsub == 1:
            body(0)
        else:
            lax.fori_loop(0, nsub, lambda t, carry: (body(t), carry)[1], 0)

    @pl.when(ik == 0)
    def _():
        m_ref[...] = jnp.full_like(m_ref, -jnp.inf)
        l_ref[...] = jnp.zeros_like(l_ref)
        acc_ref[...] = jnp.zeros_like(acc_ref)

    def step(mask_tail, running_max):
        k = k_ref[...]
        vt = v_ref[...].T
        lane = lax.broadcasted_iota(jnp.int32, k.shape, 1)
        halves = (jnp.where(lane < ATT_HEAD_DIM, k, jnp.zeros_like(k)),
                  jnp.where(lane >= ATT_HEAD_DIM, k, jnp.zeros_like(k)))
        if mask_tail:
            key_ok = (ik * tk + lax.broadcasted_iota(jnp.int32, (tk, 1), 0)) < seq_len
        kb = _divisor_tile(tk, ATT_KEY_BLOCK, V7X_MXU_DIM) if (running_max and tk % V7X_MXU_DIM == 0) else tk
        skew = ATT_SKEW_RUNNING_MAX if running_max else ATT_SKEW_DIRECT
        chains = [(slice(r0, r0 + kb), c0, min(qb, sub - c0), c)
                  for r0 in range(0, tk, kb) for c0 in range(0, sub, qb) for c in (0, 1)]

        def sub_tile(t):
            base = t * sub if nsub == 1 else pl.multiple_of(t * sub, sub)

            def scores(keys, c0, width, c):
                return _dot_nt(halves[c][keys, :], q_ref[pl.ds(base + c0, width), :])

            def softmax_update(keys, c0, width, c, st):
                cols = slice(c0, c0 + width)
                if mask_tail:
                    st = jnp.where(key_ok[keys, :], st, -jnp.inf)
                if not running_max:
                    pt = jnp.exp2(st)
                    l_ref[t, c, :, cols] += jnp.sum(pt, axis=0, keepdims=True)
                    acc_ref[t, c, :, cols] += _dot(vt[:, keys], pt.astype(BF16))
                    return
                m_prev = m_ref[t, c, :, cols]
                m_new = jnp.maximum(m_prev, jnp.max(st, axis=0, keepdims=True))
                alpha = jnp.exp2(m_prev - m_new)
                pt = jnp.exp2(st - m_new)
                l_ref[t, c, :, cols] = alpha * l_ref[t, c, :, cols] + jnp.sum(pt, axis=0, keepdims=True)
                acc_ref[t, c, :, cols] = alpha * acc_ref[t, c, :, cols] + _dot(vt[:, keys], pt.astype(BF16))
                m_ref[t, c, :, cols] = m_new

            pending = {}
            for i in range(len(chains) + skew):
                if i < len(chains):
                    pending[i] = scores(*chains[i])
                if i >= skew:
                    softmax_update(*chains[i - skew], pending.pop(i - skew))

        for_each_sub_tile(sub_tile)

    def step_either_form(mask_tail):
        pl.when(direct)(lambda: step(mask_tail, running_max=False))
        pl.when(jnp.logical_not(direct))(lambda: step(mask_tail, running_max=True))

    if nk * tk == seq_len:
        step_either_form(False)
    elif nk == 1:
        step_either_form(True)
    else:
        pl.when(ik < nk - 1)(lambda: step_either_form(False))
        pl.when(ik == nk - 1)(lambda: step_either_form(True))

    @pl.when(ik == nk - 1)
    def _():
        lp = lam_ref[...]
        a1 = jnp.sum(lp[0:1] * lp[1:2], axis=-1, keepdims=True)
        a2 = jnp.sum(lp[2:3] * lp[3:4], axis=-1, keepdims=True)
        lam = jnp.exp(a1) - jnp.exp(a2) + lam_init

        def finish(t):
            base = t * sub if nsub == 1 else pl.multiple_of(t * sub, sub)
            ot = acc_ref[t, 0] / l_ref[t, 0] - lam * (acc_ref[t, 1] / l_ref[t, 1])
            ms = jnp.mean(ot * ot, axis=0, keepdims=True)
            ot = ot * (lax.rsqrt(ms + RMS_EPS) * (1.0 - lam_init))
            o_ref[pl.ds(base, sub), :] = (ot.T * sn_ref[...]).astype(BF16)

        for_each_sub_tile(finish)


def _attn(q, k, v, layer, seq_len, plan, prep, lam_init):
    b, lp, _ = q.shape
    tq, tk, sub = plan["tq"], plan["tk"], plan["sub"]
    nk = lp // tk
    nsub = tq // sub
    return pl.pallas_call(
        functools.partial(_attn_kernel, seq_len, sub, nsub, tk, ATT_QUERY_BLOCK, nk, layer, lam_init),
        grid=(b, ATT_HEADS, lp // tq, nk),
        in_specs=[
            pl.BlockSpec(memory_space=pltpu.SMEM),
            pl.BlockSpec((None, tq, ATT_V_DIM), lambda bb, hh, iq, ik: (bb, iq, hh)),
            pl.BlockSpec((None, tk, ATT_V_DIM), lambda bb, hh, iq, ik: (bb, ik, hh)),
            pl.BlockSpec((None, tk, ATT_V_DIM), lambda bb, hh, iq, ik: (bb, ik, hh)),
            pl.BlockSpec((None, 4, ATT_HEAD_DIM), lambda bb, hh, iq, ik: (layer, 0, 0)),
            pl.BlockSpec((None, 1, ATT_V_DIM), lambda bb, hh, iq, ik: (layer, 0, 0)),
        ],
        out_specs=pl.BlockSpec((None, tq, ATT_V_DIM), lambda bb, hh, iq, ik: (bb, iq, hh)),
        out_shape=jax.ShapeDtypeStruct((b, lp, ATT_WIDTH), BF16),
        scratch_shapes=[pltpu.VMEM((nsub, 2, 1, sub), F32), pltpu.VMEM((nsub, 2, 1, sub), F32),
                        pltpu.VMEM((nsub, 2, ATT_V_DIM, sub), F32)],
        compiler_params=_params(("parallel", "parallel", "parallel", "arbitrary")),
        name="attn",
    )(prep["score_bound"], q, k, v, prep["lam"], prep["sub_norm"])


class _GlaScan:
    def __init__(self, reverse, nchunk, q_ref, k_ref, v_ref, g_ref, o_ref):
        self.reverse, self.nchunk = reverse, nchunk
        self.q_ref, self.k_ref, self.v_ref, self.g_ref, self.o_ref = q_ref, k_ref, v_ref, g_ref, o_ref
        c_rows = GLA_CHUNK
        ti = lax.broadcasted_iota(jnp.int32, (2 * c_rows, c_rows), 0)
        si = lax.broadcasted_iota(jnp.int32, (2 * c_rows, c_rows), 1)
        t_loc = ti % c_rows
        incl = self.before(si, t_loc)
        blk = jnp.logical_and(self.before(si // GLA_SUB, t_loc // GLA_SUB), (si // GLA_SUB) != (t_loc // GLA_SUB))
        self.cum_mat = jnp.where(ti < c_rows, jnp.where(incl, 1.0, 0.0), jnp.where(blk, 1.0, 0.0)).astype(BF16)
        hr = lax.broadcasted_iota(jnp.int32, (GLA_WIDTH, GLA_WIDTH), 0) // GLA_HEAD_DIM
        hc = lax.broadcasted_iota(jnp.int32, (GLA_WIDTH, GLA_WIDTH), 1) // GLA_HEAD_DIM
        self.head_mask = hr == hc
        tt = lax.broadcasted_iota(jnp.int32, (c_rows, GLA_WIDTH), 0)
        ss = lax.broadcasted_iota(jnp.int32, (c_rows, GLA_WIDTH), 1) % c_rows
        self.causal = self.before(ss, tt)
        self.row_blk = lax.broadcasted_iota(jnp.int32, (c_rows, 1), 0) // GLA_SUB

    def before(self, a, b_):
        return (a >= b_) if self.reverse else (a <= b_)

    def rows_of(self, ci):
        c = (self.nchunk - 1 - ci) if self.reverse else ci
        return slice(c * GLA_CHUNK, (c + 1) * GLA_CHUNK)

    def stage_cumsum(self, ci):
        g_hi, g_lo = _split2(self.g_ref[self.rows_of(ci), :])
        return _dot(self.cum_mat, g_hi) + _dot(self.cum_mat, g_lo)

    def stage_intra(self, ci, cums):
        rows = self.rows_of(ci)
        q = self.q_ref[rows, :].astype(F32)
        k = self.k_ref[rows, :].astype(F32)
        b_incl = cums[:GLA_CHUNK]
        b_ref = cums[GLA_CHUNK:]
        b_last = b_incl[0:1] if self.reverse else b_incl[GLA_CHUNK - 1:GLA_CHUNK]
        qs = q * jnp.exp(b_incl - b_ref)
        qe = (q * jnp.exp(b_incl)).astype(BF16)
        kd = (k * jnp.exp(b_last - b_incl)).astype(BF16)
        lhs_parts, rhs_parts = [], []
        for i in range(GLA_NSUB):
            r_i = b_ref[GLA_SUB * i:GLA_SUB * i + 1]
            k_i = jnp.where(self.before(self.row_blk, i),
                            k * jnp.exp(jnp.minimum(r_i - b_incl, GLA_EXP_CLAMP)), 0.0)
            k_i4 = jnp.concatenate([k_i] * GLA_HEADS, axis=0)
            rhs_parts.append(jnp.where(self.head_mask, k_i4, 0.0).astype(BF16))
            lhs_parts.append(jnp.where(self.row_blk == i, qs, 0.0).astype(BF16))
        lhs = jnp.concatenate(lhs_parts, axis=1)
        rhs = jnp.concatenate(rhs_parts, axis=1)
        a = jnp.where(self.causal, _dot_nt(lhs, rhs), 0.0).astype(BF16)
        return a, qe, kd, jnp.exp(b_last)

    def stage_state(self, ci, st, a, qe, kd, decay):
        rows = self.rows_of(ci)
        v = self.v_ref[rows, :]
        v4 = jnp.where(self.head_mask, jnp.concatenate([v] * GLA_HEADS, axis=0), jnp.zeros((), BF16))
        self.o_ref[rows, :] = _dot(a, v4) + _dot_nt(qe, st.astype(BF16))
        return st * decay + jnp.where(self.head_mask, _dot_tn(v, kd), 0.0)


def _gla_kernel(nchunk, qf_ref, kf_ref, vf_ref, gf_ref, qb_ref, kb_ref, vb_ref, gb_ref, of_ref, ob_ref,
                stf_ref, stb_ref):
    @pl.when(pl.program_id(1) == 0)
    def _():
        stf_ref[...] = jnp.zeros_like(stf_ref)
        stb_ref[...] = jnp.zeros_like(stb_ref)

    scans = (_GlaScan(False, nchunk, qf_ref, kf_ref, vf_ref, gf_ref, of_ref),
             _GlaScan(True, nchunk, qb_ref, kb_ref, vb_ref, gb_ref, ob_ref))
    states = [stf_ref[...], stb_ref[...]]
    cums, intra = [{}, {}], [{}, {}]
    for step in range(nchunk + 2):
        for d, scan in enumerate(scans):
            if step < nchunk:
                cums[d][step] = scan.stage_cumsum(step)
        for d, scan in enumerate(scans):
            if 0 <= step - 1 < nchunk:
                intra[d][step - 1] = scan.stage_intra(step - 1, cums[d].pop(step - 1))
        for d, scan in enumerate(scans):
            if 0 <= step - 2 < nchunk:
                states[d] = scan.stage_state(step - 2, states[d], *intra[d].pop(step - 2))
    stf_ref[...] = states[0]
    stb_ref[...] = states[1]


def _gla(gq, gk, gv, g_fwd, g_bwd, plan):
    b, lp, _ = gq.shape
    tt = plan["gla_tile"]
    nt = lp // tt
    fwd = pl.BlockSpec((None, tt, GLA_WIDTH), lambda bb, i: (bb, i, 0))
    bwd = pl.BlockSpec((None, tt, GLA_WIDTH), lambda bb, i: (bb, nt - 1 - i, 0))
    out = jax.ShapeDtypeStruct((b, lp, GLA_WIDTH), F32)
    return pl.pallas_call(
        functools.partial(_gla_kernel, tt // GLA_CHUNK),
        grid=(b, nt),
        in_specs=[fwd, fwd, fwd, fwd, bwd, bwd, bwd, bwd],
        out_specs=[fwd, bwd],
        out_shape=[out, out],
        scratch_shapes=[pltpu.VMEM((GLA_WIDTH, GLA_WIDTH), F32), pltpu.VMEM((GLA_WIDTH, GLA_WIDTH), F32)],
        compiler_params=_params(("parallel", "arbitrary")),
        name="gla",
    )(gq, gk, gv, g_fwd, gq, gk, gv, g_bwd)


def _post_kernel(seq_len, tm, nt, h_ref, oa_ref, pu_ref, pup_ref, pun_ref, of_ref, ob_ref, gg_ref,
                 pw_ref, ps_ref, gn_ref, bd_ref, wo_ref, o_ref):
    i = pl.program_id(1)
    cur = pu_ref[...]
    prev = jnp.where(i > 0, pup_ref[...], 0.0)
    nxt = jnp.where(i < nt - 1, pun_ref[...], 0.0)
    ext = jnp.concatenate([prev, cur, nxt], axis=0)
    n = tm + 2 * POOL_HALO
    p2 = ext + pltpu.roll(ext, 1, 0)
    p4 = p2 + pltpu.roll(p2, 2, 0)
    p8 = p4 + pltpu.roll(p4, 4, 0)
    p16 = p8 + pltpu.roll(p8, 8, 0)
    sums = (p2, pltpu.roll(p4, n - 1, 0), pltpu.roll(p8, n - 3, 0), pltpu.roll(p16, n - 7, 0))

    pos = i * tm + lax.broadcasted_iota(jnp.int32, (tm, 1), 0)
    grp = lax.broadcasted_iota(jnp.int32, (tm, POOL_WIDTH), 1) // POOL_GROUP_DIM
    wsum = jnp.zeros((tm, POOL_WIDTH), F32)
    cnt = jnp.ones((tm, POOL_WIDTH), F32)
    for gi, win in enumerate(POOL_WINDOWS):
        lo = win // 2
        hi = win - lo - 1
        c = jnp.minimum(pos + hi + 1, seq_len) - jnp.maximum(pos - lo, 0)
        c = jnp.maximum(c, 1).astype(F32)
        sel = grp == gi
        wsum = jnp.where(sel, sums[gi][POOL_HALO:POOL_HALO + tm], wsum)
        cnt = jnp.where(sel, c, cnt)
    pooled = (wsum / cnt - cur).astype(BF16)
    y_pool = (_dot(pooled, pw_ref[...]) * ps_ref[...]).astype(BF16)

    og = of_ref[...] + ob_ref[...]
    og = og * lax.rsqrt(_group_mean_sq(og, bd_ref[...]) + RMS_EPS) * gn_ref[...]
    gate = gg_ref[...]
    y_gla = (og * (gate * jax.nn.sigmoid(gate))).astype(BF16)

    mixed = (_dot(oa_ref[...], wo_ref[0:ATT_WIDTH, :])
             + _dot(y_pool, wo_ref[ATT_WIDTH:ATT_WIDTH + POOL_WIDTH, :])
             + _dot(y_gla, wo_ref[ATT_WIDTH + POOL_WIDTH:MIX_WIDTH, :]))
    o_ref[...] = h_ref[...] + mixed


def _post(h, o_att, pu, o_f, o_b, gg, layer, seq_len, plan, prep):
    b, lp, _ = h.shape
    tm = plan["row_tile"]
    nt = lp // tm
    hb = tm // POOL_HALO
    row = lambda w: pl.BlockSpec((None, tm, w), lambda bb, i: (bb, i, 0))
    par = lambda *shape: pl.BlockSpec((None,) + shape, lambda bb, i: (layer,) + (0,) * len(shape))
    return pl.pallas_call(
        functools.partial(_post_kernel, seq_len, tm, nt),
        grid=(b, nt),
        in_specs=[
            row(D_MODEL),
            row(ATT_WIDTH),
            row(POOL_WIDTH),
            pl.BlockSpec((None, POOL_HALO, POOL_WIDTH), lambda bb, i: (bb, jnp.maximum(i * hb - 1, 0), 0)),
            pl.BlockSpec((None, POOL_HALO, POOL_WIDTH),
                         lambda bb, i: (bb, jnp.minimum((i + 1) * hb, lp // POOL_HALO - 1), 0)),
            row(GLA_WIDTH),
            row(GLA_WIDTH),
            row(GLA_WIDTH),
            par(POOL_WIDTH, POOL_WIDTH),
            par(1, POOL_WIDTH),
            par(1, GLA_WIDTH),
            pl.BlockSpec((V7X_MXU_DIM, V7X_MXU_DIM), lambda bb, i: (0, 0)),
            par(MIX_WIDTH, D_MODEL),
        ],
        out_specs=row(D_MODEL),
        out_shape=jax.ShapeDtypeStruct((b, lp, D_MODEL), F32),
        compiler_params=_params(("parallel", "parallel")),
        name="post",
    )(h, o_att, pu, pu, pu, o_f, o_b, gg, prep["pool_w"], prep["pool_scale"], prep["gla_norm"], prep["bd"],
      prep["w_out"])


def _rope_tables(lp):
    pos = jnp.arange(lp, dtype=F32)
    inv_freq = ROPE_THETA ** (-jnp.arange(0, ATT_HEAD_DIM, 2, dtype=F32) / ATT_HEAD_DIM)
    ang = pos[:, None] * inv_freq[None, :]
    cos, sin = jnp.cos(ang), jnp.sin(ang)
    return jnp.tile(cos, (1, 4)), jnp.tile(jnp.concatenate([-sin, sin], axis=1), (1, 2))


def _prepare(p, padded_lengths):
    eye = jnp.eye(V7X_MXU_DIM // ATT_HEAD_DIM, dtype=F32)
    bd = jnp.kron(eye, jnp.full((ATT_HEAD_DIM, ATT_HEAD_DIM), 1.0 / ATT_HEAD_DIM, F32)).astype(BF16)
    w_in = jnp.pad(p["w_in"], ((0, 0), (0, 0), (0, IN_COLS_PAD - IN_COLS))).astype(BF16)
    gu = jnp.zeros((DEPTH, V7X_LANES, 2 * GLA_WIDTH), F32)
    gu = gu.at[:, :GLA_GATE_RANK, :GLA_WIDTH].set(p["gla_gate_up"][:, 0])
    gu = gu.at[:, GLA_GATE_RANK:2 * GLA_GATE_RANK, GLA_WIDTH:].set(p["gla_gate_up"][:, 1])
    pool_w = jnp.zeros((DEPTH, POOL_WIDTH, POOL_WIDTH), F32)
    for gi in range(len(POOL_WINDOWS)):
        sl = slice(gi * POOL_GROUP_DIM, (gi + 1) * POOL_GROUP_DIM)
        pool_w = pool_w.at[:, sl, sl].set(p["pool_w"][:, gi])
    tables = {lp: _rope_tables(lp) for lp in padded_lengths}
    return dict(
        bd=bd,
        w_in=w_in,
        w_out=p["w_out"].astype(BF16),
        mix_norm=p["mix_norm"][:, None, :],
        q_gain=jnp.tile(p["attn_q_norm"], (1, 2))[:, None, :],
        k_gain=jnp.tile(p["attn_k_norm"], (1, 2))[:, None, :],
        lam=jnp.stack([p["lambda_q1"], p["lambda_k1"], p["lambda_q2"], p["lambda_k2"]], axis=1),
        sub_norm=p["attn_sub_norm"][:, None, :],
        score_bound=(1.02 * ATT_HEAD_DIM * ATT_HEAD_DIM ** -0.5 * LOG2E
                     * jnp.max(jnp.abs(p["attn_q_norm"]), axis=1) * jnp.max(jnp.abs(p["attn_k_norm"]), axis=1)),
        gate_up=gu,
        gate_bias=p["gla_gate_bias"].reshape(DEPTH, 1, 2 * GLA_WIDTH),
        pool_w=pool_w.astype(BF16),
        pool_scale=p["pool_scale"][:, None, :],
        gla_norm=jnp.tile(p["gla_out_norm"], (1, GLA_HEADS))[:, None, :],
        ffn1_norm=p["ffn1_norm"][:, None, :],
        ffn2_norm=p["ffn2_norm"][:, None, :],
        ffn1_w_in=p["ffn1_w_in"].astype(BF16),
        ffn1_w_out=p["ffn1_w_out"].astype(BF16),
        ffn2_w_in=p["ffn2_w_in"].astype(BF16),
        ffn2_w_out=p["ffn2_w_out"].astype(BF16),
        cos={lp: t[0] for lp, t in tables.items()},
        sin={lp: t[1] for lp, t in tables.items()},
    )


def _trunk(x, meta_tokens, prep):
    b, s, _ = x.shape
    seq_len = s + N_META
    plan = _plan(seq_len)
    lp = plan["lp"]
    for layer in range(DEPTH):
        lam_init = 0.8 - 0.6 * math.exp(-0.3 * layer)
        if layer == 0:
            h = _ffn_embed(x, meta_tokens.astype(x.dtype), lp, layer, prep["ffn1_norm"], prep["ffn1_w_in"],
                           prep["ffn1_w_out"])
        else:
            h = _ffn(h.reshape(b * lp, D_MODEL), layer, prep["ffn1_norm"], prep["ffn1_w_in"],
                     prep["ffn1_w_out"]).reshape(b, lp, D_MODEL)
        q, k, v, pu, gq, gk, gv, gg, gf, gb = _proj(h, layer, seq_len, plan, prep)
        o_att = _attn(q, k, v, layer, seq_len, plan, prep, lam_init)
        o_f, o_b = _gla(gq, gk, gv, gf, gb, plan)
        h = _post(h, o_att, pu, o_f, o_b, gg, layer, seq_len, plan, prep)
        if layer == DEPTH - 1:
            return _ffn_unpad(h, s, layer, prep["ffn2_norm"], prep["ffn2_w_in"], prep["ffn2_w_out"])
        h = _ffn(h.reshape(b * lp, D_MODEL), layer, prep["ffn2_norm"], prep["ffn2_w_in"],
                 prep["ffn2_w_out"]).reshape(b, lp, D_MODEL)


def kernel(x_prompt, x_sample, meta_tokens, ffn1_norm, ffn1_w_in, ffn1_w_out, mix_norm, w_in, w_out, attn_q_norm,
           attn_k_norm, lambda_q1, lambda_k1, lambda_q2, lambda_k2, attn_sub_norm, pool_w, pool_scale, gla_gate_up,
           gla_gate_bias, gla_out_norm, ffn2_norm, ffn2_w_in, ffn2_w_out):
    p = dict(ffn1_norm=ffn1_norm, ffn1_w_in=ffn1_w_in, ffn1_w_out=ffn1_w_out, mix_norm=mix_norm, w_in=w_in,
             w_out=w_out, attn_q_norm=attn_q_norm, attn_k_norm=attn_k_norm, lambda_q1=lambda_q1,
             lambda_k1=lambda_k1, lambda_q2=lambda_q2, lambda_k2=lambda_k2, attn_sub_norm=attn_sub_norm,
             pool_w=pool_w, pool_scale=pool_scale, gla_gate_up=gla_gate_up, gla_gate_bias=gla_gate_bias,
             gla_out_norm=gla_out_norm, ffn2_norm=ffn2_norm, ffn2_w_in=ffn2_w_in, ffn2_w_out=ffn2_w_out)
    lengths = {_plan(x.shape[1] + N_META)["lp"] for x in (x_prompt, x_sample)}
    prep = _prepare(p, lengths)
    return (_trunk(x_prompt, meta_tokens, prep), _trunk(x_sample, meta_tokens, prep))
```

```python
import functools
import math

import jax
import jax.numpy as jnp
from jax import lax
from jax.experimental import pallas as pl
from jax.experimental.pallas import tpu as pltpu

F32 = jnp.float32
BF16 = jnp.bfloat16

D_MODEL = 1024
D_FF = 2816
DEPTH = 4
N_META = 16
RMS_EPS = 1e-6
ROPE_THETA = 10000.0

ATT_HEADS = 4
ATT_HEAD_DIM = 64
ATT_V_DIM = 2 * ATT_HEAD_DIM
ATT_WIDTH = ATT_HEADS * ATT_V_DIM

POOL_WINDOWS = (2, 4, 8, 16)
POOL_GROUP_DIM = 64
POOL_WIDTH = len(POOL_WINDOWS) * POOL_GROUP_DIM
POOL_HALO = 8

GLA_HEADS = 4
GLA_HEAD_DIM = 64
GLA_WIDTH = GLA_HEADS * GLA_HEAD_DIM
GLA_GATE_RANK = 16
GLA_GATE_TEMP = 16.0
GLA_CHUNK = 64
GLA_SUB = 16
GLA_NSUB = GLA_CHUNK // GLA_SUB
GLA_FACTORED_MAX_EXP = 80.0

MIX_WIDTH = ATT_WIDTH + POOL_WIDTH + GLA_WIDTH
IN_COLS = 4 * ATT_WIDTH - ATT_WIDTH + POOL_WIDTH + 4 * GLA_WIDTH + 2 * GLA_GATE_RANK
GATE_COL0 = IN_COLS - 2 * GLA_GATE_RANK
IN_COLS_PAD = GATE_COL0 + 128

V7X_LANES = 128
V7X_MXU_DIM = 256
SEQ_ALIGN = 128
SHORT_SEQ_ROWS = 2304
ATT_Q_ROWS_CAP = 16640
ATT_KV_TILE_LONG = 1280
ATT_QUERY_BLOCK = V7X_MXU_DIM
ATT_KEY_BLOCK = V7X_MXU_DIM
ATT_SKEW_RUNNING_MAX = 5
ATT_SKEW_DIRECT = 2
ATT_DIRECT_BOUND = 90.0
ROW_TILE_CAP = 704
GLA_TILE_CAP = 1088
FFN_TILE_CAP = 704
VMEM_LIMIT_BYTES = 56 * 1024 * 1024
FF_TILE = 256
LOG2E = 1.4426950408889634


def _divisor_tile(n, cap, mult):
    best = None
    for t in range(mult, min(n, cap) + 1, mult):
        if n % t == 0:
            best = t
    assert best is not None, (n, cap, mult)
    return best


def _plan(seq_len):
    if seq_len <= SHORT_SEQ_ROWS:
        lp = -(-seq_len // SEQ_ALIGN) * SEQ_ALIGN
        tk = tq = sub = lp
    else:
        tk = sub = ATT_KV_TILE_LONG
        lp = -(-seq_len // tk) * tk
        tq = _divisor_tile(lp, ATT_Q_ROWS_CAP, sub)
    return dict(
        lp=lp,
        row_tile=_divisor_tile(lp, ROW_TILE_CAP, 16),
        tq=tq,
        tk=tk,
        sub=sub,
        gla_tile=_divisor_tile(lp, GLA_TILE_CAP, GLA_CHUNK),
    )


def _params(sem):
    return pltpu.CompilerParams(dimension_semantics=sem, vmem_limit_bytes=VMEM_LIMIT_BYTES)


def _dot(a, b):
    return jnp.dot(a, b, preferred_element_type=F32)


def _dot_nt(a, b):
    return lax.dot_general(a, b, (((1,), (1,)), ((), ())), preferred_element_type=F32)


def _dot_tn(a, b):
    return lax.dot_general(a, b, (((0,), (0,)), ((), ())), preferred_element_type=F32)


def _split2(x):
    hi = x.astype(BF16)
    lo = (x - hi.astype(F32)).astype(BF16)
    return hi, lo


def _group_mean_sq(x, bd):
    sq = (x * x).astype(BF16)
    outs = []
    for c in range(x.shape[1] // V7X_MXU_DIM):
        sl = slice(V7X_MXU_DIM * c, V7X_MXU_DIM * (c + 1))
        outs.append(_dot(sq[:, sl], bd))
    return outs[0] if len(outs) == 1 else jnp.concatenate(outs, axis=1)


def _ffn_rows(x, gain, wi_ref, wo_ref):
    ms = jnp.mean(x * x, axis=-1, keepdims=True)
    xn = (x * lax.rsqrt(ms + RMS_EPS) * gain).astype(BF16)
    acc = None
    for c0 in range(0, D_FF, FF_TILE):
        g = _dot(xn, wi_ref[:, c0:c0 + FF_TILE])
        u = _dot(xn, wi_ref[:, D_FF + c0:D_FF + c0 + FF_TILE])
        a = (g * jax.nn.sigmoid(g) * u).astype(BF16)
        d = _dot(a, wo_ref[c0:c0 + FF_TILE, :])
        acc = d if acc is None else acc + d
    return x + 0.5 * acc


def _ffn_kernel(x_ref, g_ref, wi_ref, wo_ref, o_ref):
    o_ref[...] = _ffn_rows(x_ref[...], g_ref[...], wi_ref, wo_ref)


def _ffn_embed_kernel(n_tokens, tm, xa_ref, xb_ref, meta_ref, g_ref, wi_ref, wo_ref, o_ref):
    i = pl.program_id(1)
    x = jnp.concatenate([xa_ref[tm - N_META:, :], xb_ref[:tm - N_META, :]], axis=0)
    pos = i * tm + lax.broadcasted_iota(jnp.int32, (tm, 1), 0)
    x = jnp.where(jnp.logical_and(pos >= N_META, pos < N_META + n_tokens), x, 0.0)
    meta_rows = jnp.concatenate([meta_ref[...], jnp.zeros((tm - N_META, D_MODEL), F32)], axis=0)
    x = jnp.where(pos < N_META, meta_rows, x)
    o_ref[...] = _ffn_rows(x, g_ref[...], wi_ref, wo_ref)


def _ffn_unpad_kernel(ha_ref, hb_ref, g_ref, wi_ref, wo_ref, o_ref):
    x = jnp.concatenate([ha_ref[N_META:, :], hb_ref[:N_META, :]], axis=0)
    o_ref[...] = _ffn_rows(x, g_ref[...], wi_ref, wo_ref)


def _resident(layer, *shape):
    return pl.BlockSpec((None,) + shape, lambda *_: (layer,) + (0,) * len(shape), pipeline_mode=pl.Buffered(1))


def _ffn(h2d, layer, norm, w_in, w_out):
    rows = h2d.shape[0]
    tm = _divisor_tile(rows, FFN_TILE_CAP, 16)
    resident = functools.partial(_resident, layer)
    return pl.pallas_call(
        _ffn_kernel,
        grid=(rows // tm,),
        in_specs=[
            pl.BlockSpec((tm, D_MODEL), lambda i: (i, 0)),
            pl.BlockSpec((None, 1, D_MODEL), lambda i: (layer, 0, 0)),
            resident(D_MODEL, 2 * D_FF),
            resident(D_FF, D_MODEL),
        ],
        out_specs=pl.BlockSpec((tm, D_MODEL), lambda i: (i, 0)),
        out_shape=jax.ShapeDtypeStruct((rows, D_MODEL), F32),
        compiler_params=_params(("parallel",)),
        name="ffn",
    )(h2d, norm, w_in, w_out)


def _ffn_embed(x, meta, lp, layer, norm, w_in, w_out):
    b, n_tokens, _ = x.shape
    tm = _divisor_tile(lp, FFN_TILE_CAP, 16)
    last = -(-n_tokens // tm) - 1
    resident = functools.partial(_resident, layer)
    return pl.pallas_call(
        functools.partial(_ffn_embed_kernel, n_tokens, tm),
        grid=(b, lp // tm),
        in_specs=[
            pl.BlockSpec((None, tm, D_MODEL), lambda bb, i: (bb, jnp.clip(i - 1, 0, last), 0)),
            pl.BlockSpec((None, tm, D_MODEL), lambda bb, i: (bb, jnp.minimum(i, last), 0)),
            pl.BlockSpec((N_META, D_MODEL), lambda bb, i: (0, 0)),
            pl.BlockSpec((None, 1, D_MODEL), lambda bb, i: (layer, 0, 0)),
            resident(D_MODEL, 2 * D_FF),
            resident(D_FF, D_MODEL),
        ],
        out_specs=pl.BlockSpec((None, tm, D_MODEL), lambda bb, i: (bb, i, 0)),
        out_shape=jax.ShapeDtypeStruct((b, lp, D_MODEL), F32),
        compiler_params=_params(("parallel", "parallel")),
        name="ffn_embed",
    )(x, x, meta, norm, w_in, w_out)


def _ffn_unpad(h, n_tokens, layer, norm, w_in, w_out):
    b = h.shape[0]
    tm = _divisor_tile(n_tokens, FFN_TILE_CAP, 16)
    resident = functools.partial(_resident, layer)
    return pl.pallas_call(
        _ffn_unpad_kernel,
        grid=(b, n_tokens // tm),
        in_specs=[
            pl.BlockSpec((None, tm, D_MODEL), lambda bb, j: (bb, j, 0)),
            pl.BlockSpec((None, tm, D_MODEL), lambda bb, j: (bb, j + 1, 0)),
            pl.BlockSpec((None, 1, D_MODEL), lambda bb, j: (layer, 0, 0)),
            resident(D_MODEL, 2 * D_FF),
            resident(D_FF, D_MODEL),
        ],
        out_specs=pl.BlockSpec((None, tm, D_MODEL), lambda bb, j: (bb, j, 0)),
        out_shape=jax.ShapeDtypeStruct((b, n_tokens, D_MODEL), F32),
        compiler_params=_params(("parallel", "parallel")),
        name="ffn_unpad",
    )(h, h, norm, w_in, w_out)


def _rope128(x, cos, sin_signed, first_half):
    partner = jnp.where(first_half, pltpu.roll(x, 96, 1), pltpu.roll(x, 32, 1))
    return x * cos + partner * sin_signed


def _proj_kernel(seq_len, tm, h_ref, g_ref, w_ref, cos_ref, sin_ref, qg_ref, kg_ref, bd_ref, gu_ref, gbias_ref,
                 q_ref, k_ref, v_ref, pu_ref, gq_ref, gk_ref, gv_ref, gg_ref, gf_ref, gb_ref):
    i = pl.program_id(1)
    x = h_ref[...]
    pos = i * tm + lax.broadcasted_iota(jnp.int32, (tm, 1), 0)
    valid = pos < seq_len
    ms = jnp.mean(x * x, axis=-1, keepdims=True)
    xn = jnp.where(valid, x * lax.rsqrt(ms + RMS_EPS) * g_ref[...], 0.0).astype(BF16)

    bd = bd_ref[...]
    cos = cos_ref[...]
    sin = sin_ref[...]
    lane = lax.broadcasted_iota(jnp.int32, (tm, V7X_LANES), 1)
    first_half = (lane % ATT_HEAD_DIM) < (ATT_HEAD_DIM // 2)

    def qk_path(col0, gain_ref, out_ref, out_scale):
        z = _dot(xn, w_ref[:, col0:col0 + ATT_WIDTH])
        zn = z * lax.rsqrt(_group_mean_sq(z, bd) + RMS_EPS)
        gain = gain_ref[...]
        for c in range(ATT_WIDTH // V7X_LANES):
            sl = slice(V7X_LANES * c, V7X_LANES * (c + 1))
            r = _rope128(zn[:, sl] * gain, cos, sin, first_half)
            out_ref[:, sl] = (r * out_scale).astype(BF16)

    qk_path(0, qg_ref, q_ref, ATT_HEAD_DIM ** -0.5 * LOG2E)
    qk_path(ATT_WIDTH, kg_ref, k_ref, 1.0)

    v_ref[...] = _dot(xn, w_ref[:, 2 * ATT_WIDTH:3 * ATT_WIDTH]).astype(BF16)

    c0 = 3 * ATT_WIDTH
    pu_ref[...] = _dot(xn, w_ref[:, c0:c0 + POOL_WIDTH])
    c0 += POOL_WIDTH
    gq_ref[...] = (_dot(xn, w_ref[:, c0:c0 + GLA_WIDTH]) * (GLA_HEAD_DIM ** -0.5)).astype(BF16)
    c0 += GLA_WIDTH
    gk_ref[...] = _dot(xn, w_ref[:, c0:c0 + GLA_WIDTH]).astype(BF16)
    c0 += GLA_WIDTH
    gv_ref[...] = _dot(xn, w_ref[:, c0:c0 + GLA_WIDTH]).astype(BF16)
    c0 += GLA_WIDTH
    gg_ref[...] = _dot(xn, w_ref[:, c0:c0 + GLA_WIDTH])

    glr = _dot(xn, w_ref[:, GATE_COL0:IN_COLS_PAD])
    glr_hi, glr_lo = _split2(glr)
    gu_hi, gu_lo = _split2(gu_ref[...])
    logit = _dot(glr_hi, gu_hi) + _dot(glr_lo, gu_hi) + _dot(glr_hi, gu_lo) + gbias_ref[...]
    log_sig = jnp.minimum(logit, 0.0) - jnp.log(1.0 + jnp.exp(-jnp.abs(logit)))
    log_a = log_sig * (1.0 / GLA_GATE_TEMP)
    gf_ref[...] = log_a[:, :GLA_WIDTH]
    gb_ref[...] = log_a[:, GLA_WIDTH:]


def _proj(h, layer, seq_len, plan, prep):
    b, lp, _ = h.shape
    tm = plan["row_tile"]
    row = lambda w: pl.BlockSpec((None, tm, w), lambda bb, i: (bb, i, 0))
    par = lambda *shape: pl.BlockSpec((None,) + shape, lambda bb, i: (layer,) + (0,) * len(shape))
    out_widths = [(ATT_WIDTH, BF16), (ATT_WIDTH, BF16), (ATT_WIDTH, BF16), (POOL_WIDTH, F32),
                  (GLA_WIDTH, BF16), (GLA_WIDTH, BF16), (GLA_WIDTH, BF16), (GLA_WIDTH, F32),
                  (GLA_WIDTH, F32), (GLA_WIDTH, F32)]
    return pl.pallas_call(
        functools.partial(_proj_kernel, seq_len, tm),
        grid=(b, lp // tm),
        in_specs=[
            row(D_MODEL),
            par(1, D_MODEL),
            par(D_MODEL, IN_COLS_PAD),
            pl.BlockSpec((tm, V7X_LANES), lambda bb, i: (i, 0)),
            pl.BlockSpec((tm, V7X_LANES), lambda bb, i: (i, 0)),
            par(1, V7X_LANES),
            par(1, V7X_LANES),
            pl.BlockSpec((V7X_MXU_DIM, V7X_MXU_DIM), lambda bb, i: (0, 0)),
            par(V7X_LANES, 2 * GLA_WIDTH),
            par(1, 2 * GLA_WIDTH),
        ],
        out_specs=[row(w) for w, _ in out_widths],
        out_shape=[jax.ShapeDtypeStruct((b, lp, w), dt) for w, dt in out_widths],
        compiler_params=_params(("parallel", "parallel")),
        name="proj",
    )(h, prep["mix_norm"], prep["w_in"], prep["cos"][lp], prep["sin"][lp], prep["q_gain"], prep["k_gain"],
      prep["bd"], prep["gate_up"], prep["gate_bias"])


def _attn_kernel(seq_len, sub, nsub, tk, qb, nk, layer, lam_init, bound_ref, q_ref, k_ref, v_ref, lam_ref, sn_ref,
                 o_ref, m_ref, l_ref, acc_ref):
    ik = pl.program_id(3)
    direct = bound_ref[layer] <= ATT_DIRECT_BOUND

    def for_each_sub_tile(body):
        if nsub == 1:
            body(0)
        else:
            lax.fori_loop(0, nsub, lambda t, carry: (body(t), carry)[1], 0)

    @pl.when(ik == 0)
    def _():
        m_ref[...] = jnp.full_like(m_ref, -jnp.inf)
        l_ref[...] = jnp.zeros_like(l_ref)
        acc_ref[...] = jnp.zeros_like(acc_ref)

    def step(mask_tail, running_max):
        k = k_ref[...]
        vt = v_ref[...].T
        lane = lax.broadcasted_iota(jnp.int32, k.shape, 1)
        halves = (jnp.where(lane < ATT_HEAD_DIM, k, jnp.zeros_like(k)),
                  jnp.where(lane >= ATT_HEAD_DIM, k, jnp.zeros_like(k)))
        if mask_tail:
            key_ok = (ik * tk + lax.broadcasted_iota(jnp.int32, (tk, 1), 0)) < seq_len
        kb = _divisor_tile(tk, ATT_KEY_BLOCK, V7X_MXU_DIM) if (running_max and tk % V7X_MXU_DIM == 0) else tk
        skew = ATT_SKEW_RUNNING_MAX if running_max else ATT_SKEW_DIRECT
        chains = [(slice(r0, r0 + kb), c0, min(qb, sub - c0), c)
                  for r0 in range(0, tk, kb) for c0 in range(0, sub, qb) for c in (0, 1)]

        def sub_tile(t):
            base = t * sub if nsub == 1 else pl.multiple_of(t * sub, sub)

            def scores(keys, c0, width, c):
                return _dot_nt(halves[c][keys, :], q_ref[pl.ds(base + c0, width), :])

            def softmax_update(keys, c0, width, c, st):
                cols = slice(c0, c0 + width)
                if mask_tail:
                    st = jnp.where(key_ok[keys, :], st, -jnp.inf)
                if not running_max:
                    pt = jnp.exp2(st)
                    l_ref[t, c, :, cols] += jnp.sum(pt, axis=0, keepdims=True)
                    acc_ref[t, c, :, cols] += _dot(vt[:, keys], pt.astype(BF16))
                    return
                m_prev = m_ref[t, c, :, cols]
                m_new = jnp.maximum(m_prev, jnp.max(st, axis=0, keepdims=True))
                alpha = jnp.exp2(m_prev - m_new)
                pt = jnp.exp2(st - m_new)
                l_ref[t, c, :, cols] = alpha * l_ref[t, c, :, cols] + jnp.sum(pt, axis=0, keepdims=True)
                acc_ref[t, c, :, cols] = alpha * acc_ref[t, c, :, cols] + _dot(vt[:, keys], pt.astype(BF16))
                m_ref[t, c, :, cols] = m_new

            pending = {}
            for i in range(len(chains) + skew):
                if i < len(chains):
                    pending[i] = scores(*chains[i])
                if i >= skew:
                    softmax_update(*chains[i - skew], pending.pop(i - skew))

        for_each_sub_tile(sub_tile)

    def step_either_form(mask_tail):
        pl.when(direct)(lambda: step(mask_tail, running_max=False))
        pl.when(jnp.logical_not(direct))(lambda: step(mask_tail, running_max=True))

    if nk * tk == seq_len:
        step_either_form(False)
    elif nk == 1:
        step_either_form(True)
    else:
        pl.when(ik < nk - 1)(lambda: step_either_form(False))
        pl.when(ik == nk - 1)(lambda: step_either_form(True))

    @pl.when(ik == nk - 1)
    def _():
        lp = lam_ref[...]
        a1 = jnp.sum(lp[0:1] * lp[1:2], axis=-1, keepdims=True)
        a2 = jnp.sum(lp[2:3] * lp[3:4], axis=-1, keepdims=True)
        lam = jnp.exp(a1) - jnp.exp(a2) + lam_init

        def finish(t):
            base = t * sub if nsub == 1 else pl.multiple_of(t * sub, sub)
            ot = acc_ref[t, 0] / l_ref[t, 0] - lam * (acc_ref[t, 1] / l_ref[t, 1])
            ms = jnp.mean(ot * ot, axis=0, keepdims=True)
            ot = ot * (lax.rsqrt(ms + RMS_EPS) * (1.0 - lam_init))
            o_ref[pl.ds(base, sub), :] = (ot.T * sn_ref[...]).astype(BF16)

        for_each_sub_tile(finish)


def _attn(q, k, v, layer, seq_len, plan, prep, lam_init):
    b, lp, _ = q.shape
    tq, tk, sub = plan["tq"], plan["tk"], plan["sub"]
    nk = lp // tk
    nsub = tq // sub
    return pl.pallas_call(
        functools.partial(_attn_kernel, seq_len, sub, nsub, tk, ATT_QUERY_BLOCK, nk, layer, lam_init),
        grid=(b, ATT_HEADS, lp // tq, nk),
        in_specs=[
            pl.BlockSpec(memory_space=pltpu.SMEM),
            pl.BlockSpec((None, tq, ATT_V_DIM), lambda bb, hh, iq, ik: (bb, iq, hh)),
            pl.BlockSpec((None, tk, ATT_V_DIM), lambda bb, hh, iq, ik: (bb, ik, hh)),
            pl.BlockSpec((None, tk, ATT_V_DIM), lambda bb, hh, iq, ik: (bb, ik, hh)),
            pl.BlockSpec((None, 4, ATT_HEAD_DIM), lambda bb, hh, iq, ik: (layer, 0, 0)),
            pl.BlockSpec((None, 1, ATT_V_DIM), lambda bb, hh, iq, ik: (layer, 0, 0)),
        ],
        out_specs=pl.BlockSpec((None, tq, ATT_V_DIM), lambda bb, hh, iq, ik: (bb, iq, hh)),
        out_shape=jax.ShapeDtypeStruct((b, lp, ATT_WIDTH), BF16),
        scratch_shapes=[pltpu.VMEM((nsub, 2, 1, sub), F32), pltpu.VMEM((nsub, 2, 1, sub), F32),
                        pltpu.VMEM((nsub, 2, ATT_V_DIM, sub), F32)],
        compiler_params=_params(("parallel", "parallel", "parallel", "arbitrary")),
        name="attn",
    )(prep["score_bound"], q, k, v, prep["lam"], prep["sub_norm"])


class _GlaScan:
    def __init__(self, reverse, factored, nchunk, q_ref, k_ref, v_ref, g_ref, o_ref):
        self.reverse, self.factored, self.nchunk = reverse, factored, nchunk
        self.q_ref, self.k_ref, self.v_ref, self.g_ref, self.o_ref = q_ref, k_ref, v_ref, g_ref, o_ref
        c_rows = GLA_CHUNK
        ti = lax.broadcasted_iota(jnp.int32, (2 * c_rows, c_rows), 0)
        si = lax.broadcasted_iota(jnp.int32, (2 * c_rows, c_rows), 1)
        t_loc = ti % c_rows
        incl = self.before(si, t_loc)
        blk = jnp.logical_and(self.before(si // GLA_SUB, t_loc // GLA_SUB), (si // GLA_SUB) != (t_loc // GLA_SUB))
        self.cum_mat = jnp.where(ti < c_rows, jnp.where(incl, 1.0, 0.0), jnp.where(blk, 1.0, 0.0)).astype(BF16)
        hr = lax.broadcasted_iota(jnp.int32, (GLA_WIDTH, GLA_WIDTH), 0) // GLA_HEAD_DIM
        hc = lax.broadcasted_iota(jnp.int32, (GLA_WIDTH, GLA_WIDTH), 1) // GLA_HEAD_DIM
        self.head_mask = hr == hc
        tt = lax.broadcasted_iota(jnp.int32, (c_rows, GLA_WIDTH), 0)
        ss = lax.broadcasted_iota(jnp.int32, (c_rows, GLA_WIDTH), 1) % c_rows
        self.causal = self.before(ss, tt)
        self.row_blk = lax.broadcasted_iota(jnp.int32, (c_rows, 1), 0) // GLA_SUB
        if not factored:
            pair = (c_rows, c_rows, GLA_WIDTH)
            self.pair_mask = self.before(lax.broadcasted_iota(jnp.int32, pair, 1),
                                         lax.broadcasted_iota(jnp.int32, pair, 0))
            self.head_ones = jnp.where(self.head_mask, 1.0, 0.0).astype(BF16)

    def before(self, a, b_):
        return (a >= b_) if self.reverse else (a <= b_)

    def rows_of(self, ci):
        c = (self.nchunk - 1 - ci) if self.reverse else ci
        return slice(c * GLA_CHUNK, (c + 1) * GLA_CHUNK)

    def stage_cumsum(self, ci):
        g_hi, g_lo = _split2(self.g_ref[self.rows_of(ci), :])
        return _dot(self.cum_mat, g_hi) + _dot(self.cum_mat, g_lo)

    def stage_intra(self, ci, cums):
        rows = self.rows_of(ci)
        q = self.q_ref[rows, :].astype(F32)
        k = self.k_ref[rows, :].astype(F32)
        b_incl = cums[:GLA_CHUNK]
        b_ref = cums[GLA_CHUNK:]
        b_last = b_incl[0:1] if self.reverse else b_incl[GLA_CHUNK - 1:GLA_CHUNK]
        qe = (q * jnp.exp(b_incl)).astype(BF16)
        kd = (k * jnp.exp(b_last - b_incl)).astype(BF16)
        if not self.factored:
            decay = jnp.exp(jnp.minimum(b_incl[:, None, :] - b_incl[None, :, :], 0.0))
            x = jnp.where(self.pair_mask, q[:, None, :] * k[None, :, :] * decay, 0.0)
            y = _dot(x.reshape(GLA_CHUNK * GLA_CHUNK, GLA_WIDTH).astype(BF16), self.head_ones)
            v = self.v_ref[rows, :].astype(F32)
            o_intra = jnp.sum(y.reshape(GLA_CHUNK, GLA_CHUNK, GLA_WIDTH) * v[None, :, :], axis=1)
            return o_intra, qe, kd, jnp.exp(b_last)
        qs = q * jnp.exp(b_incl - b_ref)
        lhs_parts, rhs_parts = [], []
        for i in range(GLA_NSUB):
            r_i = b_ref[GLA_SUB * i:GLA_SUB * i + 1]
            k_i = jnp.where(self.before(self.row_blk, i), k * jnp.exp(r_i - b_incl), 0.0)
            k_i4 = jnp.concatenate([k_i] * GLA_HEADS, axis=0)
            rhs_parts.append(jnp.where(self.head_mask, k_i4, 0.0).astype(BF16))
            lhs_parts.append(jnp.where(self.row_blk == i, qs, 0.0).astype(BF16))
        lhs = jnp.concatenate(lhs_parts, axis=1)
        rhs = jnp.concatenate(rhs_parts, axis=1)
        a = jnp.where(self.causal, _dot_nt(lhs, rhs), 0.0).astype(BF16)
        return a, qe, kd, jnp.exp(b_last)

    def stage_state(self, ci, st, intra, qe, kd, decay):
        rows = self.rows_of(ci)
        v = self.v_ref[rows, :]
        if self.factored:
            v4 = jnp.where(self.head_mask, jnp.concatenate([v] * GLA_HEADS, axis=0), jnp.zeros((), BF16))
            intra = _dot(intra, v4)
        self.o_ref[rows, :] = intra + _dot_nt(qe, st.astype(BF16))
        return st * decay + jnp.where(self.head_mask, _dot_tn(v, kd), 0.0)


def _gla_kernel(nchunk, qf_ref, kf_ref, vf_ref, gf_ref, qb_ref, kb_ref, vb_ref, gb_ref, of_ref, ob_ref,
                stf_ref, stb_ref):
    @pl.when(pl.program_id(1) == 0)
    def _():
        stf_ref[...] = jnp.zeros_like(stf_ref)
        stb_ref[...] = jnp.zeros_like(stb_ref)

    def run(factored):
        scans = (_GlaScan(False, factored, nchunk, qf_ref, kf_ref, vf_ref, gf_ref, of_ref),
                 _GlaScan(True, factored, nchunk, qb_ref, kb_ref, vb_ref, gb_ref, ob_ref))
        states = [stf_ref[...], stb_ref[...]]
        cums, intra = [{}, {}], [{}, {}]
        for step in range(nchunk + 2):
            for d, scan in enumerate(scans):
                if step < nchunk:
                    cums[d][step] = scan.stage_cumsum(step)
            for d, scan in enumerate(scans):
                if 0 <= step - 1 < nchunk:
                    intra[d][step - 1] = scan.stage_intra(step - 1, cums[d].pop(step - 1))
            for d, scan in enumerate(scans):
                if 0 <= step - 2 < nchunk:
                    states[d] = scan.stage_state(step - 2, states[d], *intra[d].pop(step - 2))
        stf_ref[...] = states[0]
        stb_ref[...] = states[1]

    g_max = jnp.maximum(jnp.max(jnp.abs(gf_ref[...])), jnp.max(jnp.abs(gb_ref[...])))
    factored_ok = g_max * GLA_SUB <= GLA_FACTORED_MAX_EXP
    pl.when(factored_ok)(lambda: run(True))
    pl.when(jnp.logical_not(factored_ok))(lambda: run(False))


def _gla(gq, gk, gv, g_fwd, g_bwd, plan):
    b, lp, _ = gq.shape
    tt = plan["gla_tile"]
    nt = lp // tt
    fwd = pl.BlockSpec((None, tt, GLA_WIDTH), lambda bb, i: (bb, i, 0))
    bwd = pl.BlockSpec((None, tt, GLA_WIDTH), lambda bb, i: (bb, nt - 1 - i, 0))
    out = jax.ShapeDtypeStruct((b, lp, GLA_WIDTH), F32)
    return pl.pallas_call(
        functools.partial(_gla_kernel, tt // GLA_CHUNK),
        grid=(b, nt),
        in_specs=[fwd, fwd, fwd, fwd, bwd, bwd, bwd, bwd],
        out_specs=[fwd, bwd],
        out_shape=[out, out],
        scratch_shapes=[pltpu.VMEM((GLA_WIDTH, GLA_WIDTH), F32), pltpu.VMEM((GLA_WIDTH, GLA_WIDTH), F32)],
        compiler_params=_params(("parallel", "arbitrary")),
        name="gla",
    )(gq, gk, gv, g_fwd, gq, gk, gv, g_bwd)


def _post_kernel(seq_len, tm, nt, h_ref, oa_ref, pu_ref, pup_ref, pun_ref, of_ref, ob_ref, gg_ref,
                 pw_ref, ps_ref, gn_ref, bd_ref, wo_ref, o_ref):
    i = pl.program_id(1)
    cur = pu_ref[...]
    prev = jnp.where(i > 0, pup_ref[...], 0.0)
    nxt = jnp.where(i < nt - 1, pun_ref[...], 0.0)
    ext = jnp.concatenate([prev, cur, nxt], axis=0)
    n = tm + 2 * POOL_HALO
    p2 = ext + pltpu.roll(ext, 1, 0)
    p4 = p2 + pltpu.roll(p2, 2, 0)
    p8 = p4 + pltpu.roll(p4, 4, 0)
    p16 = p8 + pltpu.roll(p8, 8, 0)
    sums = (p2, pltpu.roll(p4, n - 1, 0), pltpu.roll(p8, n - 3, 0), pltpu.roll(p16, n - 7, 0))

    pos = i * tm + lax.broadcasted_iota(jnp.int32, (tm, 1), 0)
    grp = lax.broadcasted_iota(jnp.int32, (tm, POOL_WIDTH), 1) // POOL_GROUP_DIM
    wsum = jnp.zeros((tm, POOL_WIDTH), F32)
    cnt = jnp.ones((tm, POOL_WIDTH), F32)
    for gi, win in enumerate(POOL_WINDOWS):
        lo = win // 2
        hi = win - lo - 1
        c = jnp.minimum(pos + hi + 1, seq_len) - jnp.maximum(pos - lo, 0)
        c = jnp.maximum(c, 1).astype(F32)
        sel = grp == gi
        wsum = jnp.where(sel, sums[gi][POOL_HALO:POOL_HALO + tm], wsum)
        cnt = jnp.where(sel, c, cnt)
    pooled = (wsum / cnt - cur).astype(BF16)
    y_pool = (_dot(pooled, pw_ref[...]) * ps_ref[...]).astype(BF16)

    og = of_ref[...] + ob_ref[...]
    og = og * lax.rsqrt(_group_mean_sq(og, bd_ref[...]) + RMS_EPS) * gn_ref[...]
    gate = gg_ref[...]
    y_gla = (og * (gate * jax.nn.sigmoid(gate))).astype(BF16)

    mixed = (_dot(oa_ref[...], wo_ref[0:ATT_WIDTH, :])
             + _dot(y_pool, wo_ref[ATT_WIDTH:ATT_WIDTH + POOL_WIDTH, :])
             + _dot(y_gla, wo_ref[ATT_WIDTH + POOL_WIDTH:MIX_WIDTH, :]))
    o_ref[...] = h_ref[...] + mixed


def _post(h, o_att, pu, o_f, o_b, gg, layer, seq_len, plan, prep):
    b, lp, _ = h.shape
    tm = plan["row_tile"]
    nt = lp // tm
    hb = tm // POOL_HALO
    row = lambda w: pl.BlockSpec((None, tm, w), lambda bb, i: (bb, i, 0))
    par = lambda *shape: pl.BlockSpec((None,) + shape, lambda bb, i: (layer,) + (0,) * len(shape))
    return pl.pallas_call(
        functools.partial(_post_kernel, seq_len, tm, nt),
        grid=(b, nt),
        in_specs=[
            row(D_MODEL),
            row(ATT_WIDTH),
            row(POOL_WIDTH),
            pl.BlockSpec((None, POOL_HALO, POOL_WIDTH), lambda bb, i: (bb, jnp.maximum(i * hb - 1, 0), 0)),
            pl.BlockSpec((None, POOL_HALO, POOL_WIDTH),
                         lambda bb, i: (bb, jnp.minimum((i + 1) * hb, lp // POOL_HALO - 1), 0)),
            row(GLA_WIDTH),
            row(GLA_WIDTH),
            row(GLA_WIDTH),
            par(POOL_WIDTH, POOL_WIDTH),
            par(1, POOL_WIDTH),
            par(1, GLA_WIDTH),
            pl.BlockSpec((V7X_MXU_DIM, V7X_MXU_DIM), lambda bb, i: (0, 0)),
            par(MIX_WIDTH, D_MODEL),
        ],
        out_specs=row(D_MODEL),
        out_shape=jax.ShapeDtypeStruct((b, lp, D_MODEL), F32),
        compiler_params=_params(("parallel", "parallel")),
        name="post",
    )(h, o_att, pu, pu, pu, o_f, o_b, gg, prep["pool_w"], prep["pool_scale"], prep["gla_norm"], prep["bd"],
      prep["w_out"])


def _rope_tables(lp):
    pos = jnp.arange(lp, dtype=F32)
    inv_freq = ROPE_THETA ** (-jnp.arange(0, ATT_HEAD_DIM, 2, dtype=F32) / ATT_HEAD_DIM)
    ang = pos[:, None] * inv_freq[None, :]
    cos, sin = jnp.cos(ang), jnp.sin(ang)
    return jnp.tile(cos, (1, 4)), jnp.tile(jnp.concatenate([-sin, sin], axis=1), (1, 2))


def _prepare(p, padded_lengths):
    eye = jnp.eye(V7X_MXU_DIM // ATT_HEAD_DIM, dtype=F32)
    bd = jnp.kron(eye, jnp.full((ATT_HEAD_DIM, ATT_HEAD_DIM), 1.0 / ATT_HEAD_DIM, F32)).astype(BF16)
    w_in = jnp.pad(p["w_in"], ((0, 0), (0, 0), (0, IN_COLS_PAD - IN_COLS))).astype(BF16)
    gu = jnp.zeros((DEPTH, V7X_LANES, 2 * GLA_WIDTH), F32)
    gu = gu.at[:, :GLA_GATE_RANK, :GLA_WIDTH].set(p["gla_gate_up"][:, 0])
    gu = gu.at[:, GLA_GATE_RANK:2 * GLA_GATE_RANK, GLA_WIDTH:].set(p["gla_gate_up"][:, 1])
    pool_w = jnp.zeros((DEPTH, POOL_WIDTH, POOL_WIDTH), F32)
    for gi in range(len(POOL_WINDOWS)):
        sl = slice(gi * POOL_GROUP_DIM, (gi + 1) * POOL_GROUP_DIM)
        pool_w = pool_w.at[:, sl, sl].set(p["pool_w"][:, gi])
    tables = {lp: _rope_tables(lp) for lp in padded_lengths}
    return dict(
        bd=bd,
        w_in=w_in,
        w_out=p["w_out"].astype(BF16),
        mix_norm=p["mix_norm"][:, None, :],
        q_gain=jnp.tile(p["attn_q_norm"], (1, 2))[:, None, :],
        k_gain=jnp.tile(p["attn_k_norm"], (1, 2))[:, None, :],
        lam=jnp.stack([p["lambda_q1"], p["lambda_k1"], p["lambda_q2"], p["lambda_k2"]], axis=1),
        sub_norm=p["attn_sub_norm"][:, None, :],
        score_bound=(1.02 * ATT_HEAD_DIM * ATT_HEAD_DIM ** -0.5 * LOG2E
                     * jnp.max(jnp.abs(p["attn_q_norm"]), axis=1) * jnp.max(jnp.abs(p["attn_k_norm"]), axis=1)),
        gate_up=gu,
        gate_bias=p["gla_gate_bias"].reshape(DEPTH, 1, 2 * GLA_WIDTH),
        pool_w=pool_w.astype(BF16),
        pool_scale=p["pool_scale"][:, None, :],
        gla_norm=jnp.tile(p["gla_out_norm"], (1, GLA_HEADS))[:, None, :],
        ffn1_norm=p["ffn1_norm"][:, None, :],
        ffn2_norm=p["ffn2_norm"][:, None, :],
        ffn1_w_in=p["ffn1_w_in"].astype(BF16),
        ffn1_w_out=p["ffn1_w_out"].astype(BF16),
        ffn2_w_in=p["ffn2_w_in"].astype(BF16),
        ffn2_w_out=p["ffn2_w_out"].astype(BF16),
        cos={lp: t[0] for lp, t in tables.items()},
        sin={lp: t[1] for lp, t in tables.items()},
    )


def _trunk(x, meta_tokens, prep):
    b, s, _ = x.shape
    seq_len = s + N_META
    plan = _plan(seq_len)
    lp = plan["lp"]
    for layer in range(DEPTH):
        lam_init = 0.8 - 0.6 * math.exp(-0.3 * layer)
        if layer == 0:
            h = _ffn_embed(x, meta_tokens.astype(x.dtype), lp, layer, prep["ffn1_norm"], prep["ffn1_w_in"],
                           prep["ffn1_w_out"])
        else:
            h = _ffn(h.reshape(b * lp, D_MODEL), layer, prep["ffn1_norm"], prep["ffn1_w_in"],
                     prep["ffn1_w_out"]).reshape(b, lp, D_MODEL)
        q, k, v, pu, gq, gk, gv, gg, gf, gb = _proj(h, layer, seq_len, plan, prep)
        o_att = _attn(q, k, v, layer, seq_len, plan, prep, lam_init)
        o_f, o_b = _gla(gq, gk, gv, gf, gb, plan)
        h = _post(h, o_att, pu, o_f, o_b, gg, layer, seq_len, plan, prep)
        if layer == DEPTH - 1:
            return _ffn_unpad(h, s, layer, prep["ffn2_norm"], prep["ffn2_w_in"], prep["ffn2_w_out"])
        h = _ffn(h.reshape(b * lp, D_MODEL), layer, prep["ffn2_norm"], prep["ffn2_w_in"],
                 prep["ffn2_w_out"]).reshape(b, lp, D_MODEL)


def kernel(x_prompt, x_sample, meta_tokens, ffn1_norm, ffn1_w_in, ffn1_w_out, mix_norm, w_in, w_out, attn_q_norm,
           attn_k_norm, lambda_q1, lambda_k1, lambda_q2, lambda_k2, attn_sub_norm, pool_w, pool_scale, gla_gate_up,
           gla_gate_bias, gla_out_norm, ffn2_norm, ffn2_w_in, ffn2_w_out):
    p = dict(ffn1_norm=ffn1_norm, ffn1_w_in=ffn1_w_in, ffn1_w_out=ffn1_w_out, mix_norm=mix_norm, w_in=w_in,
             w_out=w_out, attn_q_norm=attn_q_norm, attn_k_norm=attn_k_norm, lambda_q1=lambda_q1,
             lambda_k1=lambda_k1, lambda_q2=lambda_q2, lambda_k2=lambda_k2, attn_sub_norm=attn_sub_norm,
             pool_w=pool_w, pool_scale=pool_scale, gla_gate_up=gla_gate_up, gla_gate_bias=gla_gate_bias,
             gla_out_norm=gla_out_norm, ffn2_norm=ffn2_norm, ffn2_w_in=ffn2_w_in, ffn2_w_out=ffn2_w_out)
    lengths = {_plan(x.shape[1] + N_META)["lp"] for x in (x_prompt, x_sample)}
    prep = _prepare(p, lengths)
    return (_trunk(x_prompt, meta_tokens, prep), _trunk(x_sample, meta_tokens, prep))
```

```python
import functools
import math

import jax
import jax.numpy as jnp
from jax import lax
from jax.experimental import pallas as pl
from jax.experimental.pallas import tpu as pltpu

F32 = jnp.float32
BF16 = jnp.bfloat16

D_MODEL = 1024
D_FF = 2816
DEPTH = 4
N_META = 16
RMS_EPS = 1e-6
ROPE_THETA = 10000.0

ATT_HEADS = 4
ATT_HEAD_DIM = 64
ATT_V_DIM = 2 * ATT_HEAD_DIM
ATT_WIDTH = ATT_HEADS * ATT_V_DIM

POOL_WINDOWS = (2, 4, 8, 16)
POOL_GROUP_DIM = 64
POOL_WIDTH = len(POOL_WINDOWS) * POOL_GROUP_DIM
POOL_HALO = 8

GLA_HEADS = 4
GLA_HEAD_DIM = 64
GLA_WIDTH = GLA_HEADS * GLA_HEAD_DIM
GLA_GATE_RANK = 16
GLA_GATE_TEMP = 16.0
GLA_CHUNK = 64
GLA_SUB = 16
GLA_NSUB = GLA_CHUNK // GLA_SUB
GLA_FACTORED_MAX_EXP = 80.0

MIX_WIDTH = ATT_WIDTH + POOL_WIDTH + GLA_WIDTH
IN_COLS = 4 * ATT_WIDTH - ATT_WIDTH + POOL_WIDTH + 4 * GLA_WIDTH + 2 * GLA_GATE_RANK
GATE_COL0 = IN_COLS - 2 * GLA_GATE_RANK
IN_COLS_PAD = GATE_COL0 + 128

V7X_LANES = 128
V7X_MXU_DIM = 256
SEQ_ALIGN = 128
SHORT_SEQ_ROWS = 2304
ATT_Q_ROWS_CAP = 16640
ATT_KV_TILE_LONG = 1280
ATT_QUERY_BLOCK = V7X_MXU_DIM
ATT_KEY_BLOCK = V7X_MXU_DIM
ATT_SKEW_RUNNING_MAX = 5
ATT_SKEW_DIRECT = 2
ATT_DIRECT_BOUND = 90.0
ROW_TILE_CAP = 704
GLA_TILE_CAP = 1088
FFN_TILE_CAP = 704
VMEM_LIMIT_BYTES = 56 * 1024 * 1024
FF_TILE = 256
LOG2E = 1.4426950408889634


def _divisor_tile(n, cap, mult):
    best = None
    for t in range(mult, min(n, cap) + 1, mult):
        if n % t == 0:
            best = t
    assert best is not None, (n, cap, mult)
    return best


def _plan(seq_len):
    if seq_len <= SHORT_SEQ_ROWS:
        lp = -(-seq_len // SEQ_ALIGN) * SEQ_ALIGN
        tk = tq = sub = lp
    else:
        tk = sub = ATT_KV_TILE_LONG
        lp = -(-seq_len // tk) * tk
        tq = _divisor_tile(lp, ATT_Q_ROWS_CAP, sub)
    return dict(
        lp=lp,
        row_tile=_divisor_tile(lp, ROW_TILE_CAP, 16),
        tq=tq,
        tk=tk,
        sub=sub,
        gla_tile=_divisor_tile(lp, GLA_TILE_CAP, GLA_CHUNK),
    )


def _params(sem):
    return pltpu.CompilerParams(dimension_semantics=sem, vmem_limit_bytes=VMEM_LIMIT_BYTES)


def _dot(a, b):
    return jnp.dot(a, b, preferred_element_type=F32)


def _dot_nt(a, b):
    return lax.dot_general(a, b, (((1,), (1,)), ((), ())), preferred_element_type=F32)


def _dot_tn(a, b):
    return lax.dot_general(a, b, (((0,), (0,)), ((), ())), preferred_element_type=F32)


def _split2(x):
    hi = x.astype(BF16)
    lo = (x - hi.astype(F32)).astype(BF16)
    return hi, lo


def _group_mean_sq(x, bd):
    sq = (x * x).astype(BF16)
    outs = []
    for c in range(x.shape[1] // V7X_MXU_DIM):
        sl = slice(V7X_MXU_DIM * c, V7X_MXU_DIM * (c + 1))
        outs.append(_dot(sq[:, sl], bd))
    return outs[0] if len(outs) == 1 else jnp.concatenate(outs, axis=1)


def _ffn_rows(x, gain, wi_ref, wo_ref):
    ms = jnp.mean(x * x, axis=-1, keepdims=True)
    xn = (x * lax.rsqrt(ms + RMS_EPS) * gain).astype(BF16)
    acc = None
    for c0 in range(0, D_FF, FF_TILE):
        g = _dot(xn, wi_ref[:, c0:c0 + FF_TILE])
        u = _dot(xn, wi_ref[:, D_FF + c0:D_FF + c0 + FF_TILE])
        a = (g * jax.nn.sigmoid(g) * u).astype(BF16)
        d = _dot(a, wo_ref[c0:c0 + FF_TILE, :])
        acc = d if acc is None else acc + d
    return x + 0.5 * acc


def _ffn_kernel(x_ref, g_ref, wi_ref, wo_ref, o_ref):
    o_ref[...] = _ffn_rows(x_ref[...], g_ref[...], wi_ref, wo_ref)


def _ffn_embed_kernel(n_tokens, tm, xa_ref, xb_ref, meta_ref, g_ref, wi_ref, wo_ref, o_ref):
    i = pl.program_id(1)
    x = jnp.concatenate([xa_ref[tm - N_META:, :], xb_ref[:tm - N_META, :]], axis=0)
    pos = i * tm + lax.broadcasted_iota(jnp.int32, (tm, 1), 0)
    x = jnp.where(jnp.logical_and(pos >= N_META, pos < N_META + n_tokens), x, 0.0)
    meta_rows = jnp.concatenate([meta_ref[...], jnp.zeros((tm - N_META, D_MODEL), F32)], axis=0)
    x = jnp.where(pos < N_META, meta_rows, x)
    o_ref[...] = _ffn_rows(x, g_ref[...], wi_ref, wo_ref)


def _ffn_unpad_kernel(ha_ref, hb_ref, g_ref, wi_ref, wo_ref, o_ref):
    x = jnp.concatenate([ha_ref[N_META:, :], hb_ref[:N_META, :]], axis=0)
    o_ref[...] = _ffn_rows(x, g_ref[...], wi_ref, wo_ref)


def _resident(layer, *shape):
    return pl.BlockSpec((None,) + shape, lambda *_: (layer,) + (0,) * len(shape), pipeline_mode=pl.Buffered(1))


def _ffn(h2d, layer, norm, w_in, w_out):
    rows = h2d.shape[0]
    tm = _divisor_tile(rows, FFN_TILE_CAP, 16)
    resident = functools.partial(_resident, layer)
    return pl.pallas_call(
        _ffn_kernel,
        grid=(rows // tm,),
        in_specs=[
            pl.BlockSpec((tm, D_MODEL), lambda i: (i, 0)),
            pl.BlockSpec((None, 1, D_MODEL), lambda i: (layer, 0, 0)),
            resident(D_MODEL, 2 * D_FF),
            resident(D_FF, D_MODEL),
        ],
        out_specs=pl.BlockSpec((tm, D_MODEL), lambda i: (i, 0)),
        out_shape=jax.ShapeDtypeStruct((rows, D_MODEL), F32),
        compiler_params=_params(("parallel",)),
        name="ffn",
    )(h2d, norm, w_in, w_out)


def _ffn_embed(x, meta, lp, layer, norm, w_in, w_out):
    b, n_tokens, _ = x.shape
    tm = _divisor_tile(lp, FFN_TILE_CAP, 16)
    last = -(-n_tokens // tm) - 1
    resident = functools.partial(_resident, layer)
    return pl.pallas_call(
        functools.partial(_ffn_embed_kernel, n_tokens, tm),
        grid=(b, lp // tm),
        in_specs=[
            pl.BlockSpec((None, tm, D_MODEL), lambda bb, i: (bb, jnp.clip(i - 1, 0, last), 0)),
            pl.BlockSpec((None, tm, D_MODEL), lambda bb, i: (bb, jnp.minimum(i, last), 0)),
            pl.BlockSpec((N_META, D_MODEL), lambda bb, i: (0, 0)),
            pl.BlockSpec((None, 1, D_MODEL), lambda bb, i: (layer, 0, 0)),
            resident(D_MODEL, 2 * D_FF),
            resident(D_FF, D_MODEL),
        ],
        out_specs=pl.BlockSpec((None, tm, D_MODEL), lambda bb, i: (bb, i, 0)),
        out_shape=jax.ShapeDtypeStruct((b, lp, D_MODEL), F32),
        compiler_params=_params(("parallel", "parallel")),
        name="ffn_embed",
    )(x, x, meta, norm, w_in, w_out)


def _ffn_unpad(h, n_tokens, layer, norm, w_in, w_out):
    b = h.shape[0]
    tm = _divisor_tile(n_tokens, FFN_TILE_CAP, 16)
    resident = functools.partial(_resident, layer)
    return pl.pallas_call(
        _ffn_unpad_kernel,
        grid=(b, n_tokens // tm),
        in_specs=[
            pl.BlockSpec((None, tm, D_MODEL), lambda bb, j: (bb, j, 0)),
            pl.BlockSpec((None, tm, D_MODEL), lambda bb, j: (bb, j + 1, 0)),
            pl.BlockSpec((None, 1, D_MODEL), lambda bb, j: (layer, 0, 0)),
            resident(D_MODEL, 2 * D_FF),
            resident(D_FF, D_MODEL),
        ],
        out_specs=pl.BlockSpec((None, tm, D_MODEL), lambda bb, j: (bb, j, 0)),
        out_shape=jax.ShapeDtypeStruct((b, n_tokens, D_MODEL), F32),
        compiler_params=_params(("parallel", "parallel")),
        name="ffn_unpad",
    )(h, h, norm, w_in, w_out)


def _rope128(x, cos, sin_signed, first_half):
    partner = jnp.where(first_half, pltpu.roll(x, 96, 1), pltpu.roll(x, 32, 1))
    return x * cos + partner * sin_signed


def _proj_kernel(seq_len, tm, h_ref, g_ref, w_ref, cos_ref, sin_ref, qg_ref, kg_ref, bd_ref, gu_ref, gbias_ref,
                 q_ref, k_ref, v_ref, pu_ref, gq_ref, gk_ref, gv_ref, gg_ref, gf_ref, gb_ref, gmax_ref):
    i = pl.program_id(1)
    x = h_ref[...]
    pos = i * tm + lax.broadcasted_iota(jnp.int32, (tm, 1), 0)
    valid = pos < seq_len
    ms = jnp.mean(x * x, axis=-1, keepdims=True)
    xn = jnp.where(valid, x * lax.rsqrt(ms + RMS_EPS) * g_ref[...], 0.0).astype(BF16)

    bd = bd_ref[...]
    cos = cos_ref[...]
    sin = sin_ref[...]
    lane = lax.broadcasted_iota(jnp.int32, (tm, V7X_LANES), 1)
    first_half = (lane % ATT_HEAD_DIM) < (ATT_HEAD_DIM // 2)

    def qk_path(col0, gain_ref, out_ref, out_scale):
        z = _dot(xn, w_ref[:, col0:col0 + ATT_WIDTH])
        zn = z * lax.rsqrt(_group_mean_sq(z, bd) + RMS_EPS)
        gain = gain_ref[...]
        for c in range(ATT_WIDTH // V7X_LANES):
            sl = slice(V7X_LANES * c, V7X_LANES * (c + 1))
            r = _rope128(zn[:, sl] * gain, cos, sin, first_half)
            out_ref[:, sl] = (r * out_scale).astype(BF16)

    qk_path(0, qg_ref, q_ref, ATT_HEAD_DIM ** -0.5 * LOG2E)
    qk_path(ATT_WIDTH, kg_ref, k_ref, 1.0)

    v_ref[...] = _dot(xn, w_ref[:, 2 * ATT_WIDTH:3 * ATT_WIDTH]).astype(BF16)

    c0 = 3 * ATT_WIDTH
    pu_ref[...] = _dot(xn, w_ref[:, c0:c0 + POOL_WIDTH])
    c0 += POOL_WIDTH
    gq_ref[...] = (_dot(xn, w_ref[:, c0:c0 + GLA_WIDTH]) * (GLA_HEAD_DIM ** -0.5)).astype(BF16)
    c0 += GLA_WIDTH
    gk_ref[...] = _dot(xn, w_ref[:, c0:c0 + GLA_WIDTH]).astype(BF16)
    c0 += GLA_WIDTH
    gv_ref[...] = _dot(xn, w_ref[:, c0:c0 + GLA_WIDTH]).astype(BF16)
    c0 += GLA_WIDTH
    gg_ref[...] = _dot(xn, w_ref[:, c0:c0 + GLA_WIDTH])

    glr = _dot(xn, w_ref[:, GATE_COL0:IN_COLS_PAD])
    glr_hi, glr_lo = _split2(glr)
    gu_hi, gu_lo = _split2(gu_ref[...])
    logit = _dot(glr_hi, gu_hi) + _dot(glr_lo, gu_hi) + _dot(glr_hi, gu_lo) + gbias_ref[...]
    log_sig = jnp.minimum(logit, 0.0) - jnp.log(1.0 + jnp.exp(-jnp.abs(logit)))
    log_a = log_sig * (1.0 / GLA_GATE_TEMP)
    gf_ref[...] = log_a[:, :GLA_WIDTH]
    gb_ref[...] = log_a[:, GLA_WIDTH:]
    tile_max = jnp.max(jnp.max(jnp.abs(log_a), axis=1, keepdims=True), axis=0, keepdims=True)
    gmax_ref[...] = jnp.broadcast_to(tile_max, gmax_ref.shape)


def _proj(h, layer, seq_len, plan, prep):
    b, lp, _ = h.shape
    tm = plan["row_tile"]
    row = lambda w: pl.BlockSpec((None, tm, w), lambda bb, i: (bb, i, 0))
    par = lambda *shape: pl.BlockSpec((None,) + shape, lambda bb, i: (layer,) + (0,) * len(shape))
    out_widths = [(ATT_WIDTH, BF16), (ATT_WIDTH, BF16), (ATT_WIDTH, BF16), (POOL_WIDTH, F32),
                  (GLA_WIDTH, BF16), (GLA_WIDTH, BF16), (GLA_WIDTH, BF16), (GLA_WIDTH, F32),
                  (GLA_WIDTH, F32), (GLA_WIDTH, F32)]
    return pl.pallas_call(
        functools.partial(_proj_kernel, seq_len, tm),
        grid=(b, lp // tm),
        in_specs=[
            row(D_MODEL),
            par(1, D_MODEL),
            par(D_MODEL, IN_COLS_PAD),
            pl.BlockSpec((tm, V7X_LANES), lambda bb, i: (i, 0)),
            pl.BlockSpec((tm, V7X_LANES), lambda bb, i: (i, 0)),
            par(1, V7X_LANES),
            par(1, V7X_LANES),
            pl.BlockSpec((V7X_MXU_DIM, V7X_MXU_DIM), lambda bb, i: (0, 0)),
            par(V7X_LANES, 2 * GLA_WIDTH),
            par(1, 2 * GLA_WIDTH),
        ],
        out_specs=[row(w) for w, _ in out_widths]
        + [pl.BlockSpec((None, None, 8, V7X_LANES), lambda bb, i: (bb, i, 0, 0))],
        out_shape=[jax.ShapeDtypeStruct((b, lp, w), dt) for w, dt in out_widths]
        + [jax.ShapeDtypeStruct((b, lp // tm, 8, V7X_LANES), F32)],
        compiler_params=_params(("parallel", "parallel")),
        name="proj",
    )(h, prep["mix_norm"], prep["w_in"], prep["cos"][lp], prep["sin"][lp], prep["q_gain"], prep["k_gain"],
      prep["bd"], prep["gate_up"], prep["gate_bias"])


def _attn_kernel(seq_len, sub, nsub, tk, qb, nk, layer, lam_init, bound_ref, q_ref, k_ref, v_ref, lam_ref, sn_ref,
                 o_ref, m_ref, l_ref, acc_ref):
    ik = pl.program_id(3)
    direct = bound_ref[layer] <= ATT_DIRECT_BOUND

    def for_each_sub_tile(body):
        if nsub == 1:
            body(0)
        else:
            lax.fori_loop(0, nsub, lambda t, carry: (body(t), carry)[1], 0)

    @pl.when(ik == 0)
    def _():
        m_ref[...] = jnp.full_like(m_ref, -jnp.inf)
        l_ref[...] = jnp.zeros_like(l_ref)
        acc_ref[...] = jnp.zeros_like(acc_ref)

    def step(mask_tail, running_max):
        k = k_ref[...]
        vt = v_ref[...].T
        lane = lax.broadcasted_iota(jnp.int32, k.shape, 1)
        halves = (jnp.where(lane < ATT_HEAD_DIM, k, jnp.zeros_like(k)),
                  jnp.where(lane >= ATT_HEAD_DIM, k, jnp.zeros_like(k)))
        if mask_tail:
            key_ok = (ik * tk + lax.broadcasted_iota(jnp.int32, (tk, 1), 0)) < seq_len
        kb = _divisor_tile(tk, ATT_KEY_BLOCK, V7X_MXU_DIM) if (running_max and tk % V7X_MXU_DIM == 0) else tk
        skew = ATT_SKEW_RUNNING_MAX if running_max else ATT_SKEW_DIRECT
        chains = [(slice(r0, r0 + kb), c0, min(qb, sub - c0), c)
                  for r0 in range(0, tk, kb) for c0 in range(0, sub, qb) for c in (0, 1)]

        def sub_tile(t):
            base = t * sub if nsub == 1 else pl.multiple_of(t * sub, sub)

            def scores(keys, c0, width, c):
                return _dot_nt(halves[c][keys, :], q_ref[pl.ds(base + c0, width), :])

            def softmax_update(keys, c0, width, c, st):
                cols = slice(c0, c0 + width)
                if mask_tail:
                    st = jnp.where(key_ok[keys, :], st, -jnp.inf)
                if not running_max:
                    pt = jnp.exp2(st)
                    l_ref[t, c, :, cols] += jnp.sum(pt, axis=0, keepdims=True)
                    acc_ref[t, c, :, cols] += _dot(vt[:, keys], pt.astype(BF16))
                    return
                m_prev = m_ref[t, c, :, cols]
                m_new = jnp.maximum(m_prev, jnp.max(st, axis=0, keepdims=True))
                alpha = jnp.exp2(m_prev - m_new)
                pt = jnp.exp2(st - m_new)
                l_ref[t, c, :, cols] = alpha * l_ref[t, c, :, cols] + jnp.sum(pt, axis=0, keepdims=True)
                acc_ref[t, c, :, cols] = alpha * acc_ref[t, c, :, cols] + _dot(vt[:, keys], pt.astype(BF16))
                m_ref[t, c, :, cols] = m_new

            pending = {}
            for i in range(len(chains) + skew):
                if i < len(chains):
                    pending[i] = scores(*chains[i])
                if i >= skew:
                    softmax_update(*chains[i - skew], pending.pop(i - skew))

        for_each_sub_tile(sub_tile)

    def step_either_form(mask_tail):
        pl.when(direct)(lambda: step(mask_tail, running_max=False))
        pl.when(jnp.logical_not(direct))(lambda: step(mask_tail, running_max=True))

    if nk * tk == seq_len:
        step_either_form(False)
    elif nk == 1:
        step_either_form(True)
    else:
        pl.when(ik < nk - 1)(lambda: step_either_form(False))
        pl.when(ik == nk - 1)(lambda: step_either_form(True))

    @pl.when(ik == nk - 1)
    def _():
        lp = lam_ref[...]
        a1 = jnp.sum(lp[0:1] * lp[1:2], axis=-1, keepdims=True)
        a2 = jnp.sum(lp[2:3] * lp[3:4], axis=-1, keepdims=True)
        lam = jnp.exp(a1) - jnp.exp(a2) + lam_init

        def finish(t):
            base = t * sub if nsub == 1 else pl.multiple_of(t * sub, sub)
            ot = acc_ref[t, 0] / l_ref[t, 0] - lam * (acc_ref[t, 1] / l_ref[t, 1])
            ms = jnp.mean(ot * ot, axis=0, keepdims=True)
            ot = ot * (lax.rsqrt(ms + RMS_EPS) * (1.0 - lam_init))
            o_ref[pl.ds(base, sub), :] = (ot.T * sn_ref[...]).astype(BF16)

        for_each_sub_tile(finish)


def _attn(q, k, v, layer, seq_len, plan, prep, lam_init):
    b, lp, _ = q.shape
    tq, tk, sub = plan["tq"], plan["tk"], plan["sub"]
    nk = lp // tk
    nsub = tq // sub
    return pl.pallas_call(
        functools.partial(_attn_kernel, seq_len, sub, nsub, tk, ATT_QUERY_BLOCK, nk, layer, lam_init),
        grid=(b, ATT_HEADS, lp // tq, nk),
        in_specs=[
            pl.BlockSpec(memory_space=pltpu.SMEM),
            pl.BlockSpec((None, tq, ATT_V_DIM), lambda bb, hh, iq, ik: (bb, iq, hh)),
            pl.BlockSpec((None, tk, ATT_V_DIM), lambda bb, hh, iq, ik: (bb, ik, hh)),
            pl.BlockSpec((None, tk, ATT_V_DIM), lambda bb, hh, iq, ik: (bb, ik, hh)),
            pl.BlockSpec((None, 4, ATT_HEAD_DIM), lambda bb, hh, iq, ik: (layer, 0, 0)),
            pl.BlockSpec((None, 1, ATT_V_DIM), lambda bb, hh, iq, ik: (layer, 0, 0)),
        ],
        out_specs=pl.BlockSpec((None, tq, ATT_V_DIM), lambda bb, hh, iq, ik: (bb, iq, hh)),
        out_shape=jax.ShapeDtypeStruct((b, lp, ATT_WIDTH), BF16),
        scratch_shapes=[pltpu.VMEM((nsub, 2, 1, sub), F32), pltpu.VMEM((nsub, 2, 1, sub), F32),
                        pltpu.VMEM((nsub, 2, ATT_V_DIM, sub), F32)],
        compiler_params=_params(("parallel", "parallel", "parallel", "arbitrary")),
        name="attn",
    )(prep["score_bound"], q, k, v, prep["lam"], prep["sub_norm"])


class _GlaScan:
    def __init__(self, reverse, factored, nchunk, q_ref, k_ref, v_ref, g_ref, o_ref):
        self.reverse, self.factored, self.nchunk = reverse, factored, nchunk
        self.q_ref, self.k_ref, self.v_ref, self.g_ref, self.o_ref = q_ref, k_ref, v_ref, g_ref, o_ref
        c_rows = GLA_CHUNK
        ti = lax.broadcasted_iota(jnp.int32, (2 * c_rows, c_rows), 0)
        si = lax.broadcasted_iota(jnp.int32, (2 * c_rows, c_rows), 1)
        t_loc = ti % c_rows
        incl = self.before(si, t_loc)
        blk = jnp.logical_and(self.before(si // GLA_SUB, t_loc // GLA_SUB), (si // GLA_SUB) != (t_loc // GLA_SUB))
        self.cum_mat = jnp.where(ti < c_rows, jnp.where(incl, 1.0, 0.0), jnp.where(blk, 1.0, 0.0)).astype(BF16)
        hr = lax.broadcasted_iota(jnp.int32, (GLA_WIDTH, GLA_WIDTH), 0) // GLA_HEAD_DIM
        hc = lax.broadcasted_iota(jnp.int32, (GLA_WIDTH, GLA_WIDTH), 1) // GLA_HEAD_DIM
        self.head_mask = hr == hc
        tt = lax.broadcasted_iota(jnp.int32, (c_rows, GLA_WIDTH), 0)
        ss = lax.broadcasted_iota(jnp.int32, (c_rows, GLA_WIDTH), 1) % c_rows
        self.causal = self.before(ss, tt)
        self.row_blk = lax.broadcasted_iota(jnp.int32, (c_rows, 1), 0) // GLA_SUB
        if not factored:
            pair = (c_rows, c_rows, GLA_WIDTH)
            self.pair_mask = self.before(lax.broadcasted_iota(jnp.int32, pair, 1),
                                         lax.broadcasted_iota(jnp.int32, pair, 0))
            self.head_ones = jnp.where(self.head_mask, 1.0, 0.0).astype(BF16)

    def before(self, a, b_):
        return (a >= b_) if self.reverse else (a <= b_)

    def rows_of(self, ci):
        c = (self.nchunk - 1 - ci) if self.reverse else ci
        return slice(c * GLA_CHUNK, (c + 1) * GLA_CHUNK)

    def stage_cumsum(self, ci):
        g_hi, g_lo = _split2(self.g_ref[self.rows_of(ci), :])
        return _dot(self.cum_mat, g_hi) + _dot(self.cum_mat, g_lo)

    def stage_intra(self, ci, cums):
        rows = self.rows_of(ci)
        q = self.q_ref[rows, :].astype(F32)
        k = self.k_ref[rows, :].astype(F32)
        b_incl = cums[:GLA_CHUNK]
        b_ref = cums[GLA_CHUNK:]
        b_last = b_incl[0:1] if self.reverse else b_incl[GLA_CHUNK - 1:GLA_CHUNK]
        qe = (q * jnp.exp(b_incl)).astype(BF16)
        kd = (k * jnp.exp(b_last - b_incl)).astype(BF16)
        if not self.factored:
            decay = jnp.exp(jnp.minimum(b_incl[:, None, :] - b_incl[None, :, :], 0.0))
            x = jnp.where(self.pair_mask, q[:, None, :] * k[None, :, :] * decay, 0.0)
            y = _dot(x.reshape(GLA_CHUNK * GLA_CHUNK, GLA_WIDTH).astype(BF16), self.head_ones)
            v = self.v_ref[rows, :].astype(F32)
            o_intra = jnp.sum(y.reshape(GLA_CHUNK, GLA_CHUNK, GLA_WIDTH) * v[None, :, :], axis=1)
            return o_intra, qe, kd, jnp.exp(b_last)
        qs = q * jnp.exp(b_incl - b_ref)
        lhs_parts, rhs_parts = [], []
        for i in range(GLA_NSUB):
            r_i = b_ref[GLA_SUB * i:GLA_SUB * i + 1]
            k_i = jnp.where(self.before(self.row_blk, i), k * jnp.exp(r_i - b_incl), 0.0)
            k_i4 = jnp.concatenate([k_i] * GLA_HEADS, axis=0)
            rhs_parts.append(jnp.where(self.head_mask, k_i4, 0.0).astype(BF16))
            lhs_parts.append(jnp.where(self.row_blk == i, qs, 0.0).astype(BF16))
        lhs = jnp.concatenate(lhs_parts, axis=1)
        rhs = jnp.concatenate(rhs_parts, axis=1)
        a = jnp.where(self.causal, _dot_nt(lhs, rhs), 0.0).astype(BF16)
        return a, qe, kd, jnp.exp(b_last)

    def stage_state(self, ci, st, intra, qe, kd, decay):
        rows = self.rows_of(ci)
        v = self.v_ref[rows, :]
        if self.factored:
            v4 = jnp.where(self.head_mask, jnp.concatenate([v] * GLA_HEADS, axis=0), jnp.zeros((), BF16))
            intra = _dot(intra, v4)
        self.o_ref[rows, :] = intra + _dot_nt(qe, st.astype(BF16))
        return st * decay + jnp.where(self.head_mask, _dot_tn(v, kd), 0.0)


def _gla_kernel(nchunk, gmax_ref, qf_ref, kf_ref, vf_ref, gf_ref, qb_ref, kb_ref, vb_ref, gb_ref, of_ref, ob_ref,
                stf_ref, stb_ref):
    @pl.when(pl.program_id(1) == 0)
    def _():
        stf_ref[...] = jnp.zeros_like(stf_ref)
        stb_ref[...] = jnp.zeros_like(stb_ref)

    def run(factored):
        scans = (_GlaScan(False, factored, nchunk, qf_ref, kf_ref, vf_ref, gf_ref, of_ref),
                 _GlaScan(True, factored, nchunk, qb_ref, kb_ref, vb_ref, gb_ref, ob_ref))
        states = [stf_ref[...], stb_ref[...]]
        cums, intra = [{}, {}], [{}, {}]
        for step in range(nchunk + 2):
            for d, scan in enumerate(scans):
                if step < nchunk:
                    cums[d][step] = scan.stage_cumsum(step)
            for d, scan in enumerate(scans):
                if 0 <= step - 1 < nchunk:
                    intra[d][step - 1] = scan.stage_intra(step - 1, cums[d].pop(step - 1))
            for d, scan in enumerate(scans):
                if 0 <= step - 2 < nchunk:
                    states[d] = scan.stage_state(step - 2, states[d], *intra[d].pop(step - 2))
        stf_ref[...] = states[0]
        stb_ref[...] = states[1]

    factored_ok = gmax_ref[pl.program_id(0)] * GLA_SUB <= GLA_FACTORED_MAX_EXP
    pl.when(factored_ok)(lambda: run(True))
    pl.when(jnp.logical_not(factored_ok))(lambda: run(False))


def _gla(gq, gk, gv, g_fwd, g_bwd, g_abs_max, plan):
    b, lp, _ = gq.shape
    tt = plan["gla_tile"]
    nt = lp // tt
    fwd = pl.BlockSpec((None, tt, GLA_WIDTH), lambda bb, i: (bb, i, 0))
    bwd = pl.BlockSpec((None, tt, GLA_WIDTH), lambda bb, i: (bb, nt - 1 - i, 0))
    out = jax.ShapeDtypeStruct((b, lp, GLA_WIDTH), F32)
    return pl.pallas_call(
        functools.partial(_gla_kernel, tt // GLA_CHUNK),
        grid=(b, nt),
        in_specs=[pl.BlockSpec(memory_space=pltpu.SMEM), fwd, fwd, fwd, fwd, bwd, bwd, bwd, bwd],
        out_specs=[fwd, bwd],
        out_shape=[out, out],
        scratch_shapes=[pltpu.VMEM((GLA_WIDTH, GLA_WIDTH), F32), pltpu.VMEM((GLA_WIDTH, GLA_WIDTH), F32)],
        compiler_params=_params(("parallel", "arbitrary")),
        name="gla",
    )(g_abs_max, gq, gk, gv, g_fwd, gq, gk, gv, g_bwd)


def _post_kernel(seq_len, tm, nt, h_ref, oa_ref, pu_ref, pup_ref, pun_ref, of_ref, ob_ref, gg_ref,
                 pw_ref, ps_ref, gn_ref, bd_ref, wo_ref, o_ref):
    i = pl.program_id(1)
    cur = pu_ref[...]
    prev = jnp.where(i > 0, pup_ref[...], 0.0)
    nxt = jnp.where(i < nt - 1, pun_ref[...], 0.0)
    ext = jnp.concatenate([prev, cur, nxt], axis=0)
    n = tm + 2 * POOL_HALO
    p2 = ext + pltpu.roll(ext, 1, 0)
    p4 = p2 + pltpu.roll(p2, 2, 0)
    p8 = p4 + pltpu.roll(p4, 4, 0)
    p16 = p8 + pltpu.roll(p8, 8, 0)
    sums = (p2, pltpu.roll(p4, n - 1, 0), pltpu.roll(p8, n - 3, 0), pltpu.roll(p16, n - 7, 0))

    pos = i * tm + lax.broadcasted_iota(jnp.int32, (tm, 1), 0)
    grp = lax.broadcasted_iota(jnp.int32, (tm, POOL_WIDTH), 1) // POOL_GROUP_DIM
    wsum = jnp.zeros((tm, POOL_WIDTH), F32)
    cnt = jnp.ones((tm, POOL_WIDTH), F32)
    for gi, win in enumerate(POOL_WINDOWS):
        lo = win // 2
        hi = win - lo - 1
        c = jnp.minimum(pos + hi + 1, seq_len) - jnp.maximum(pos - lo, 0)
        c = jnp.maximum(c, 1).astype(F32)
        sel = grp == gi
        wsum = jnp.where(sel, sums[gi][POOL_HALO:POOL_HALO + tm], wsum)
        cnt = jnp.where(sel, c, cnt)
    pooled = (wsum / cnt - cur).astype(BF16)
    y_pool = (_dot(pooled, pw_ref[...]) * ps_ref[...]).astype(BF16)

    og = of_ref[...] + ob_ref[...]
    og = og * lax.rsqrt(_group_mean_sq(og, bd_ref[...]) + RMS_EPS) * gn_ref[...]
    gate = gg_ref[...]
    y_gla = (og * (gate * jax.nn.sigmoid(gate))).astype(BF16)

    mixed = (_dot(oa_ref[...], wo_ref[0:ATT_WIDTH, :])
             + _dot(y_pool, wo_ref[ATT_WIDTH:ATT_WIDTH + POOL_WIDTH, :])
             + _dot(y_gla, wo_ref[ATT_WIDTH + POOL_WIDTH:MIX_WIDTH, :]))
    o_ref[...] = h_ref[...] + mixed


def _post(h, o_att, pu, o_f, o_b, gg, layer, seq_len, plan, prep):
    b, lp, _ = h.shape
    tm = plan["row_tile"]
    nt = lp // tm
    hb = tm // POOL_HALO
    row = lambda w: pl.BlockSpec((None, tm, w), lambda bb, i: (bb, i, 0))
    par = lambda *shape: pl.BlockSpec((None,) + shape, lambda bb, i: (layer,) + (0,) * len(shape))
    return pl.pallas_call(
        functools.partial(_post_kernel, seq_len, tm, nt),
        grid=(b, nt),
        in_specs=[
            row(D_MODEL),
            row(ATT_WIDTH),
            row(POOL_WIDTH),
            pl.BlockSpec((None, POOL_HALO, POOL_WIDTH), lambda bb, i: (bb, jnp.maximum(i * hb - 1, 0), 0)),
            pl.BlockSpec((None, POOL_HALO, POOL_WIDTH),
                         lambda bb, i: (bb, jnp.minimum((i + 1) * hb, lp // POOL_HALO - 1), 0)),
            row(GLA_WIDTH),
            row(GLA_WIDTH),
            row(GLA_WIDTH),
            par(POOL_WIDTH, POOL_WIDTH),
            par(1, POOL_WIDTH),
            par(1, GLA_WIDTH),
            pl.BlockSpec((V7X_MXU_DIM, V7X_MXU_DIM), lambda bb, i: (0, 0)),
            par(MIX_WIDTH, D_MODEL),
        ],
        out_specs=row(D_MODEL),
        out_shape=jax.ShapeDtypeStruct((b, lp, D_MODEL), F32),
        compiler_params=_params(("parallel", "parallel")),
        name="post",
    )(h, o_att, pu, pu, pu, o_f, o_b, gg, prep["pool_w"], prep["pool_scale"], prep["gla_norm"], prep["bd"],
      prep["w_out"])


def _rope_tables(lp):
    pos = jnp.arange(lp, dtype=F32)
    inv_freq = ROPE_THETA ** (-jnp.arange(0, ATT_HEAD_DIM, 2, dtype=F32) / ATT_HEAD_DIM)
    ang = pos[:, None] * inv_freq[None, :]
    cos, sin = jnp.cos(ang), jnp.sin(ang)
    return jnp.tile(cos, (1, 4)), jnp.tile(jnp.concatenate([-sin, sin], axis=1), (1, 2))


def _prepare(p, padded_lengths):
    eye = jnp.eye(V7X_MXU_DIM // ATT_HEAD_DIM, dtype=F32)
    bd = jnp.kron(eye, jnp.full((ATT_HEAD_DIM, ATT_HEAD_DIM), 1.0 / ATT_HEAD_DIM, F32)).astype(BF16)
    w_in = jnp.pad(p["w_in"], ((0, 0), (0, 0), (0, IN_COLS_PAD - IN_COLS))).astype(BF16)
    gu = jnp.zeros((DEPTH, V7X_LANES, 2 * GLA_WIDTH), F32)
    gu = gu.at[:, :GLA_GATE_RANK, :GLA_WIDTH].set(p["gla_gate_up"][:, 0])
    gu = gu.at[:, GLA_GATE_RANK:2 * GLA_GATE_RANK, GLA_WIDTH:].set(p["gla_gate_up"][:, 1])
    pool_w = jnp.zeros((DEPTH, POOL_WIDTH, POOL_WIDTH), F32)
    for gi in range(len(POOL_WINDOWS)):
        sl = slice(gi * POOL_GROUP_DIM, (gi + 1) * POOL_GROUP_DIM)
        pool_w = pool_w.at[:, sl, sl].set(p["pool_w"][:, gi])
    tables = {lp: _rope_tables(lp) for lp in padded_lengths}
    return dict(
        bd=bd,
        w_in=w_in,
        w_out=p["w_out"].astype(BF16),
        mix_norm=p["mix_norm"][:, None, :],
        q_gain=jnp.tile(p["attn_q_norm"], (1, 2))[:, None, :],
        k_gain=jnp.tile(p["attn_k_norm"], (1, 2))[:, None, :],
        lam=jnp.stack([p["lambda_q1"], p["lambda_k1"], p["lambda_q2"], p["lambda_k2"]], axis=1),
        sub_norm=p["attn_sub_norm"][:, None, :],
        score_bound=(1.02 * ATT_HEAD_DIM * ATT_HEAD_DIM ** -0.5 * LOG2E
                     * jnp.max(jnp.abs(p["attn_q_norm"]), axis=1) * jnp.max(jnp.abs(p["attn_k_norm"]), axis=1)),
        gate_up=gu,
        gate_bias=p["gla_gate_bias"].reshape(DEPTH, 1, 2 * GLA_WIDTH),
        pool_w=pool_w.astype(BF16),
        pool_scale=p["pool_scale"][:, None, :],
        gla_norm=jnp.tile(p["gla_out_norm"], (1, GLA_HEADS))[:, None, :],
        ffn1_norm=p["ffn1_norm"][:, None, :],
        ffn2_norm=p["ffn2_norm"][:, None, :],
        ffn1_w_in=p["ffn1_w_in"].astype(BF16),
        ffn1_w_out=p["ffn1_w_out"].astype(BF16),
        ffn2_w_in=p["ffn2_w_in"].astype(BF16),
        ffn2_w_out=p["ffn2_w_out"].astype(BF16),
        cos={lp: t[0] for lp, t in tables.items()},
        sin={lp: t[1] for lp, t in tables.items()},
    )


def _trunk(x, meta_tokens, prep):
    b, s, _ = x.shape
    seq_len = s + N_META
    plan = _plan(seq_len)
    lp = plan["lp"]
    for layer in range(DEPTH):
        lam_init = 0.8 - 0.6 * math.exp(-0.3 * layer)
        if layer == 0:
            h = _ffn_embed(x, meta_tokens.astype(x.dtype), lp, layer, prep["ffn1_norm"], prep["ffn1_w_in"],
                           prep["ffn1_w_out"])
        else:
            h = _ffn(h.reshape(b * lp, D_MODEL), layer, prep["ffn1_norm"], prep["ffn1_w_in"],
                     prep["ffn1_w_out"]).reshape(b, lp, D_MODEL)
        q, k, v, pu, gq, gk, gv, gg, gf, gb, gmax = _proj(h, layer, seq_len, plan, prep)
        o_att = _attn(q, k, v, layer, seq_len, plan, prep, lam_init)
        o_f, o_b = _gla(gq, gk, gv, gf, gb, jnp.max(gmax, axis=(1, 2, 3)), plan)
        h = _post(h, o_att, pu, o_f, o_b, gg, layer, seq_len, plan, prep)
        if layer == DEPTH - 1:
            return _ffn_unpad(h, s, layer, prep["ffn2_norm"], prep["ffn2_w_in"], prep["ffn2_w_out"])
        h = _ffn(h.reshape(b * lp, D_MODEL), layer, prep["ffn2_norm"], prep["ffn2_w_in"],
                 prep["ffn2_w_out"]).reshape(b, lp, D_MODEL)


def kernel(x_prompt, x_sample, meta_tokens, ffn1_norm, ffn1_w_in, ffn1_w_out, mix_norm, w_in, w_out, attn_q_norm,
           attn_k_norm, lambda_q1, lambda_k1, lambda_q2, lambda_k2, attn_sub_norm, pool_w, pool_scale, gla_gate_up,
           gla_gate_bias, gla_out_norm, ffn2_norm, ffn2_w_in, ffn2_w_out):
    p = dict(ffn1_norm=ffn1_norm, ffn1_w_in=ffn1_w_in, ffn1_w_out=ffn1_w_out, mix_norm=mix_norm, w_in=w_in,
             w_out=w_out, attn_q_norm=attn_q_norm, attn_k_norm=attn_k_norm, lambda_q1=lambda_q1,
             lambda_k1=lambda_k1, lambda_q2=lambda_q2, lambda_k2=lambda_k2, attn_sub_norm=attn_sub_norm,
             pool_w=pool_w, pool_scale=pool_scale, gla_gate_up=gla_gate_up, gla_gate_bias=gla_gate_bias,
             gla_out_norm=gla_out_norm, ffn2_norm=ffn2_norm, ffn2_w_in=ffn2_w_in, ffn2_w_out=ffn2_w_out)
    lengths = {_plan(x.shape[1] + N_META)["lp"] for x in (x_prompt, x_sample)}
    prep = _prepare(p, lengths)
    return (_trunk(x_prompt, meta_tokens, prep), _trunk(x_sample, meta_tokens, prep))
```

```python
import functools
import math

import jax
import jax.numpy as jnp
from jax import lax
from jax.experimental import pallas as pl
from jax.experimental.pallas import tpu as pltpu

F32 = jnp.float32
BF16 = jnp.bfloat16

D_MODEL = 1024
D_FF = 2816
DEPTH = 4
N_META = 16
RMS_EPS = 1e-6
ROPE_THETA = 10000.0

ATT_HEADS = 4
ATT_HEAD_DIM = 64
ATT_V_DIM = 2 * ATT_HEAD_DIM
ATT_WIDTH = ATT_HEADS * ATT_V_DIM

POOL_WINDOWS = (2, 4, 8, 16)
POOL_GROUP_DIM = 64
POOL_WIDTH = len(POOL_WINDOWS) * POOL_GROUP_DIM
POOL_HALO = 8

GLA_HEADS = 4
GLA_HEAD_DIM = 64
GLA_WIDTH = GLA_HEADS * GLA_HEAD_DIM
GLA_GATE_RANK = 16
GLA_GATE_TEMP = 16.0
GLA_CHUNK = 64
GLA_SUB = 16
GLA_NSUB = GLA_CHUNK // GLA_SUB
GLA_FACTORED_MAX_EXP = 80.0

MIX_WIDTH = ATT_WIDTH + POOL_WIDTH + GLA_WIDTH
IN_COLS = 4 * ATT_WIDTH - ATT_WIDTH + POOL_WIDTH + 4 * GLA_WIDTH + 2 * GLA_GATE_RANK
GATE_COL0 = IN_COLS - 2 * GLA_GATE_RANK
IN_COLS_PAD = GATE_COL0 + 128

V7X_LANES = 128
V7X_MXU_DIM = 256
SEQ_ALIGN = 128
SHORT_SEQ_ROWS = 2304
ATT_Q_ROWS_CAP = 16640
ATT_KV_TILE_LONG = 1280
ATT_QUERY_BLOCK = V7X_MXU_DIM
ATT_KEY_BLOCK = V7X_MXU_DIM
ATT_SKEW_RUNNING_MAX = 5
ATT_SKEW_DIRECT = 2
ATT_DIRECT_BOUND = 90.0
ROW_TILE_CAP = 704
GLA_TILE_CAP = 1088
FFN_TILE_CAP = 704
VMEM_LIMIT_BYTES = 56 * 1024 * 1024
FF_TILE = 256
LOG2E = 1.4426950408889634


def _divisor_tile(n, cap, mult):
    best = None
    for t in range(mult, min(n, cap) + 1, mult):
        if n % t == 0:
            best = t
    assert best is not None, (n, cap, mult)
    return best


def _plan(seq_len):
    if seq_len <= SHORT_SEQ_ROWS:
        lp = -(-seq_len // SEQ_ALIGN) * SEQ_ALIGN
        tk = tq = sub = lp
    else:
        tk = sub = ATT_KV_TILE_LONG
        lp = -(-seq_len // tk) * tk
        tq = _divisor_tile(lp, ATT_Q_ROWS_CAP, sub)
    return dict(
        lp=lp,
        row_tile=_divisor_tile(lp, ROW_TILE_CAP, 16),
        tq=tq,
        tk=tk,
        sub=sub,
        gla_tile=_divisor_tile(lp, GLA_TILE_CAP, GLA_CHUNK),
    )


def _params(sem):
    return pltpu.CompilerParams(dimension_semantics=sem, vmem_limit_bytes=VMEM_LIMIT_BYTES)


def _dot(a, b):
    return jnp.dot(a, b, preferred_element_type=F32)


def _dot_nt(a, b):
    return lax.dot_general(a, b, (((1,), (1,)), ((), ())), preferred_element_type=F32)


def _dot_tn(a, b):
    return lax.dot_general(a, b, (((0,), (0,)), ((), ())), preferred_element_type=F32)


def _split2(x):
    hi = x.astype(BF16)
    lo = (x - hi.astype(F32)).astype(BF16)
    return hi, lo


def _group_mean_sq(x, bd):
    sq = (x * x).astype(BF16)
    outs = []
    for c in range(x.shape[1] // V7X_MXU_DIM):
        sl = slice(V7X_MXU_DIM * c, V7X_MXU_DIM * (c + 1))
        outs.append(_dot(sq[:, sl], bd))
    return outs[0] if len(outs) == 1 else jnp.concatenate(outs, axis=1)


def _ffn_rows(x, gain, wi_ref, wo_ref):
    ms = jnp.mean(x * x, axis=-1, keepdims=True)
    xn = (x * lax.rsqrt(ms + RMS_EPS) * gain).astype(BF16)
    acc = None
    for c0 in range(0, D_FF, FF_TILE):
        g = _dot(xn, wi_ref[:, c0:c0 + FF_TILE])
        u = _dot(xn, wi_ref[:, D_FF + c0:D_FF + c0 + FF_TILE])
        a = (g * jax.nn.sigmoid(g) * u).astype(BF16)
        d = _dot(a, wo_ref[c0:c0 + FF_TILE, :])
        acc = d if acc is None else acc + d
    return x + 0.5 * acc


def _ffn_kernel(x_ref, g_ref, wi_ref, wo_ref, o_ref):
    o_ref[...] = _ffn_rows(x_ref[...], g_ref[...], wi_ref, wo_ref)


def _ffn_embed_kernel(n_tokens, tm, xa_ref, xb_ref, meta_ref, g_ref, wi_ref, wo_ref, o_ref):
    i = pl.program_id(1)
    x = jnp.concatenate([xa_ref[tm - N_META:, :], xb_ref[:tm - N_META, :]], axis=0)
    pos = i * tm + lax.broadcasted_iota(jnp.int32, (tm, 1), 0)
    x = jnp.where(jnp.logical_and(pos >= N_META, pos < N_META + n_tokens), x, 0.0)
    meta_rows = jnp.concatenate([meta_ref[...], jnp.zeros((tm - N_META, D_MODEL), F32)], axis=0)
    x = jnp.where(pos < N_META, meta_rows, x)
    o_ref[...] = _ffn_rows(x, g_ref[...], wi_ref, wo_ref)


def _ffn_unpad_kernel(ha_ref, hb_ref, g_ref, wi_ref, wo_ref, o_ref):
    x = jnp.concatenate([ha_ref[N_META:, :], hb_ref[:N_META, :]], axis=0)
    o_ref[...] = _ffn_rows(x, g_ref[...], wi_ref, wo_ref)


def _resident(layer, *shape):
    return pl.BlockSpec((None,) + shape, lambda *_: (layer,) + (0,) * len(shape), pipeline_mode=pl.Buffered(1))


def _ffn(h2d, layer, norm, w_in, w_out):
    rows = h2d.shape[0]
    tm = _divisor_tile(rows, FFN_TILE_CAP, 16)
    resident = functools.partial(_resident, layer)
    return pl.pallas_call(
        _ffn_kernel,
        grid=(rows // tm,),
        in_specs=[
            pl.BlockSpec((tm, D_MODEL), lambda i: (i, 0)),
            pl.BlockSpec((None, 1, D_MODEL), lambda i: (layer, 0, 0)),
            resident(D_MODEL, 2 * D_FF),
            resident(D_FF, D_MODEL),
        ],
        out_specs=pl.BlockSpec((tm, D_MODEL), lambda i: (i, 0)),
        out_shape=jax.ShapeDtypeStruct((rows, D_MODEL), F32),
        compiler_params=_params(("parallel",)),
        name="ffn",
    )(h2d, norm, w_in, w_out)


def _ffn_embed(x, meta, lp, layer, norm, w_in, w_out):
    b, n_tokens, _ = x.shape
    tm = _divisor_tile(lp, FFN_TILE_CAP, 16)
    last = -(-n_tokens // tm) - 1
    resident = functools.partial(_resident, layer)
    return pl.pallas_call(
        functools.partial(_ffn_embed_kernel, n_tokens, tm),
        grid=(b, lp // tm),
        in_specs=[
            pl.BlockSpec((None, tm, D_MODEL), lambda bb, i: (bb, jnp.clip(i - 1, 0, last), 0)),
            pl.BlockSpec((None, tm, D_MODEL), lambda bb, i: (bb, jnp.minimum(i, last), 0)),
            pl.BlockSpec((N_META, D_MODEL), lambda bb, i: (0, 0)),
            pl.BlockSpec((None, 1, D_MODEL), lambda bb, i: (layer, 0, 0)),
            resident(D_MODEL, 2 * D_FF),
            resident(D_FF, D_MODEL),
        ],
        out_specs=pl.BlockSpec((None, tm, D_MODEL), lambda bb, i: (bb, i, 0)),
        out_shape=jax.ShapeDtypeStruct((b, lp, D_MODEL), F32),
        compiler_params=_params(("parallel", "parallel")),
        name="ffn_embed",
    )(x, x, meta, norm, w_in, w_out)


def _ffn_unpad(h, n_tokens, layer, norm, w_in, w_out):
    b = h.shape[0]
    tm = _divisor_tile(n_tokens, FFN_TILE_CAP, 16)
    resident = functools.partial(_resident, layer)
    return pl.pallas_call(
        _ffn_unpad_kernel,
        grid=(b, n_tokens // tm),
        in_specs=[
            pl.BlockSpec((None, tm, D_MODEL), lambda bb, j: (bb, j, 0)),
            pl.BlockSpec((None, tm, D_MODEL), lambda bb, j: (bb, j + 1, 0)),
            pl.BlockSpec((None, 1, D_MODEL), lambda bb, j: (layer, 0, 0)),
            resident(D_MODEL, 2 * D_FF),
            resident(D_FF, D_MODEL),
        ],
        out_specs=pl.BlockSpec((None, tm, D_MODEL), lambda bb, j: (bb, j, 0)),
        out_shape=jax.ShapeDtypeStruct((b, n_tokens, D_MODEL), F32),
        compiler_params=_params(("parallel", "parallel")),
        name="ffn_unpad",
    )(h, h, norm, w_in, w_out)


def _rope128(x, cos, sin_signed, first_half):
    partner = jnp.where(first_half, pltpu.roll(x, 96, 1), pltpu.roll(x, 32, 1))
    return x * cos + partner * sin_signed


def _proj_kernel(seq_len, tm, h_ref, g_ref, w_ref, cos_ref, sin_ref, qg_ref, kg_ref, bd_ref, gu_ref, gbias_ref,
                 q_ref, k_ref, v_ref, pu_ref, gq_ref, gk_ref, gv_ref, gg_ref, gf_ref, gb_ref, gmax_ref):
    i = pl.program_id(1)
    x = h_ref[...]
    pos = i * tm + lax.broadcasted_iota(jnp.int32, (tm, 1), 0)
    valid = pos < seq_len
    ms = jnp.mean(x * x, axis=-1, keepdims=True)
    xn = jnp.where(valid, x * lax.rsqrt(ms + RMS_EPS) * g_ref[...], 0.0).astype(BF16)

    bd = bd_ref[...]
    cos = cos_ref[...]
    sin = sin_ref[...]
    lane = lax.broadcasted_iota(jnp.int32, (tm, V7X_LANES), 1)
    first_half = (lane % ATT_HEAD_DIM) < (ATT_HEAD_DIM // 2)

    def qk_path(col0, gain_ref, out_ref, out_scale):
        z = _dot(xn, w_ref[:, col0:col0 + ATT_WIDTH])
        zn = z * lax.rsqrt(_group_mean_sq(z, bd) + RMS_EPS)
        gain = gain_ref[...]
        for c in range(ATT_WIDTH // V7X_LANES):
            sl = slice(V7X_LANES * c, V7X_LANES * (c + 1))
            r = _rope128(zn[:, sl] * gain, cos, sin, first_half)
            out_ref[:, sl] = (r * out_scale).astype(BF16)

    qk_path(0, qg_ref, q_ref, ATT_HEAD_DIM ** -0.5 * LOG2E)
    qk_path(ATT_WIDTH, kg_ref, k_ref, 1.0)

    v_ref[...] = _dot(xn, w_ref[:, 2 * ATT_WIDTH:3 * ATT_WIDTH]).astype(BF16)

    c0 = 3 * ATT_WIDTH
    pu_ref[...] = _dot(xn, w_ref[:, c0:c0 + POOL_WIDTH])
    c0 += POOL_WIDTH
    gq_ref[...] = (_dot(xn, w_ref[:, c0:c0 + GLA_WIDTH]) * (GLA_HEAD_DIM ** -0.5)).astype(BF16)
    c0 += GLA_WIDTH
    gk_ref[...] = _dot(xn, w_ref[:, c0:c0 + GLA_WIDTH]).astype(BF16)
    c0 += GLA_WIDTH
    gv_ref[...] = _dot(xn, w_ref[:, c0:c0 + GLA_WIDTH]).astype(BF16)
    c0 += GLA_WIDTH
    gg_ref[...] = _dot(xn, w_ref[:, c0:c0 + GLA_WIDTH])

    glr = _dot(xn, w_ref[:, GATE_COL0:IN_COLS_PAD])
    glr_hi, glr_lo = _split2(glr)
    gu_hi, gu_lo = _split2(gu_ref[...])
    logit = _dot(glr_hi, gu_hi) + _dot(glr_lo, gu_hi) + _dot(glr_hi, gu_lo) + gbias_ref[...]
    log_sig = jnp.minimum(logit, 0.0) - jnp.log(1.0 + jnp.exp(-jnp.abs(logit)))
    log_a = log_sig * (1.0 / GLA_GATE_TEMP)
    gf_ref[...] = log_a[:, :GLA_WIDTH]
    gb_ref[...] = log_a[:, GLA_WIDTH:]
    tile_max = jnp.max(jnp.max(jnp.abs(log_a), axis=1, keepdims=True), axis=0, keepdims=True)
    gmax_ref[...] = jnp.broadcast_to(tile_max, gmax_ref.shape)


def _proj(h, layer, seq_len, plan, prep):
    b, lp, _ = h.shape
    tm = plan["row_tile"]
    row = lambda w: pl.BlockSpec((None, tm, w), lambda bb, i: (bb, i, 0))
    par = lambda *shape: pl.BlockSpec((None,) + shape, lambda bb, i: (layer,) + (0,) * len(shape))
    out_widths = [(ATT_WIDTH, BF16), (ATT_WIDTH, BF16), (ATT_WIDTH, BF16), (POOL_WIDTH, F32),
                  (GLA_WIDTH, BF16), (GLA_WIDTH, BF16), (GLA_WIDTH, BF16), (GLA_WIDTH, F32),
                  (GLA_WIDTH, F32), (GLA_WIDTH, F32)]
    return pl.pallas_call(
        functools.partial(_proj_kernel, seq_len, tm),
        grid=(b, lp // tm),
        in_specs=[
            row(D_MODEL),
            par(1, D_MODEL),
            par(D_MODEL, IN_COLS_PAD),
            pl.BlockSpec((tm, V7X_LANES), lambda bb, i: (i, 0)),
            pl.BlockSpec((tm, V7X_LANES), lambda bb, i: (i, 0)),
            par(1, V7X_LANES),
            par(1, V7X_LANES),
            pl.BlockSpec((V7X_MXU_DIM, V7X_MXU_DIM), lambda bb, i: (0, 0)),
            par(V7X_LANES, 2 * GLA_WIDTH),
            par(1, 2 * GLA_WIDTH),
        ],
        out_specs=[row(w) for w, _ in out_widths]
        + [pl.BlockSpec((None, None, 8, V7X_LANES), lambda bb, i: (bb, i, 0, 0))],
        out_shape=[jax.ShapeDtypeStruct((b, lp, w), dt) for w, dt in out_widths]
        + [jax.ShapeDtypeStruct((b, lp // tm, 8, V7X_LANES), F32)],
        compiler_params=_params(("parallel", "parallel")),
        name="proj",
    )(h, prep["mix_norm"], prep["w_in"], prep["cos"][lp], prep["sin"][lp], prep["q_gain"], prep["k_gain"],
      prep["bd"], prep["gate_up"], prep["gate_bias"])


def _attn_kernel(seq_len, sub, nsub, tk, qb, nk, layer, lam_init, bound_ref, q_ref, k_ref, v_ref, lam_ref, sn_ref,
                 o_ref, m_ref, l_ref, acc_ref):
    ik = pl.program_id(3)
    direct = bound_ref[layer] <= ATT_DIRECT_BOUND

    def for_each_sub_tile(body):
        if nsub == 1:
            body(0)
        else:
            lax.fori_loop(0, nsub, lambda t, carry: (body(t), carry)[1], 0)

    @pl.when(ik == 0)
    def _():
        m_ref[...] = jnp.full_like(m_ref, -jnp.inf)
        l_ref[...] = jnp.zeros_like(l_ref)
        acc_ref[...] = jnp.zeros_like(acc_ref)

    def step(mask_tail, running_max):
        k = k_ref[...]
        vt = v_ref[...].T
        lane = lax.broadcasted_iota(jnp.int32, k.shape, 1)
        halves = (jnp.where(lane < ATT_HEAD_DIM, k, jnp.zeros_like(k)),
                  jnp.where(lane >= ATT_HEAD_DIM, k, jnp.zeros_like(k)))
        if mask_tail:
            key_ok = (ik * tk + lax.broadcasted_iota(jnp.int32, (tk, 1), 0)) < seq_len
        kb = _divisor_tile(tk, ATT_KEY_BLOCK, V7X_MXU_DIM) if (running_max and tk % V7X_MXU_DIM == 0) else tk
        skew = ATT_SKEW_RUNNING_MAX if running_max else ATT_SKEW_DIRECT
        chains = [(slice(r0, r0 + kb), c0, min(qb, sub - c0), c)
                  for r0 in range(0, tk, kb) for c0 in range(0, sub, qb) for c in (0, 1)]

        def sub_tile(t):
            base = t * sub if nsub == 1 else pl.multiple_of(t * sub, sub)

            def scores(keys, c0, width, c):
                return _dot_nt(halves[c][keys, :], q_ref[pl.ds(base + c0, width), :])

            def softmax_update(keys, c0, width, c, st):
                cols = slice(c0, c0 + width)
                if mask_tail:
                    st = jnp.where(key_ok[keys, :], st, -jnp.inf)
                if not running_max:
                    pt = jnp.exp2(st)
                    l_ref[t, c, :, cols] += jnp.sum(pt, axis=0, keepdims=True)
                    acc_ref[t, c, :, cols] += _dot(vt[:, keys], pt.astype(BF16))
                    return
                m_prev = m_ref[t, c, :, cols]
                m_new = jnp.maximum(m_prev, jnp.max(st, axis=0, keepdims=True))
                alpha = jnp.exp2(m_prev - m_new)
                pt = jnp.exp2(st - m_new)
                l_ref[t, c, :, cols] = alpha * l_ref[t, c, :, cols] + jnp.sum(pt, axis=0, keepdims=True)
                acc_ref[t, c, :, cols] = alpha * acc_ref[t, c, :, cols] + _dot(vt[:, keys], pt.astype(BF16))
                m_ref[t, c, :, cols] = m_new

            pending = {}
            for i in range(len(chains) + skew):
                if i < len(chains):
                    pending[i] = scores(*chains[i])
                if i >= skew:
                    softmax_update(*chains[i - skew], pending.pop(i - skew))

        for_each_sub_tile(sub_tile)

    def step_either_form(mask_tail):
        pl.when(direct)(lambda: step(mask_tail, running_max=False))
        pl.when(jnp.logical_not(direct))(lambda: step(mask_tail, running_max=True))

    if nk * tk == seq_len:
        step_either_form(False)
    elif nk == 1:
        step_either_form(True)
    else:
        pl.when(ik < nk - 1)(lambda: step_either_form(False))
        pl.when(ik == nk - 1)(lambda: step_either_form(True))

    @pl.when(ik == nk - 1)
    def _():
        lp = lam_ref[...]
        a1 = jnp.sum(lp[0:1] * lp[1:2], axis=-1, keepdims=True)
        a2 = jnp.sum(lp[2:3] * lp[3:4], axis=-1, keepdims=True)
        lam = jnp.exp(a1) - jnp.exp(a2) + lam_init

        def finish(t):
            base = t * sub if nsub == 1 else pl.multiple_of(t * sub, sub)
            ot = acc_ref[t, 0] / l_ref[t, 0] - lam * (acc_ref[t, 1] / l_ref[t, 1])
            ms = jnp.mean(ot * ot, axis=0, keepdims=True)
            ot = ot * (lax.rsqrt(ms + RMS_EPS) * (1.0 - lam_init))
            o_ref[pl.ds(base, sub), :] = (ot.T * sn_ref[...]).astype(BF16)

        for_each_sub_tile(finish)


def _attn(q, k, v, layer, seq_len, plan, prep, lam_init):
    b, lp, _ = q.shape
    tq, tk, sub = plan["tq"], plan["tk"], plan["sub"]
    nk = lp // tk
    nsub = tq // sub
    return pl.pallas_call(
        functools.partial(_attn_kernel, seq_len, sub, nsub, tk, ATT_QUERY_BLOCK, nk, layer, lam_init),
        grid=(b, ATT_HEADS, lp // tq, nk),
        in_specs=[
            pl.BlockSpec(memory_space=pltpu.SMEM),
            pl.BlockSpec((None, tq, ATT_V_DIM), lambda bb, hh, iq, ik: (bb, iq, hh)),
            pl.BlockSpec((None, tk, ATT_V_DIM), lambda bb, hh, iq, ik: (bb, ik, hh)),
            pl.BlockSpec((None, tk, ATT_V_DIM), lambda bb, hh, iq, ik: (bb, ik, hh)),
            pl.BlockSpec((None, 4, ATT_HEAD_DIM), lambda bb, hh, iq, ik: (layer, 0, 0)),
            pl.BlockSpec((None, 1, ATT_V_DIM), lambda bb, hh, iq, ik: (layer, 0, 0)),
        ],
        out_specs=pl.BlockSpec((None, tq, ATT_V_DIM), lambda bb, hh, iq, ik: (bb, iq, hh)),
        out_shape=jax.ShapeDtypeStruct((b, lp, ATT_WIDTH), BF16),
        scratch_shapes=[pltpu.VMEM((nsub, 2, 1, sub), F32), pltpu.VMEM((nsub, 2, 1, sub), F32),
                        pltpu.VMEM((nsub, 2, ATT_V_DIM, sub), F32)],
        compiler_params=_params(("parallel", "parallel", "parallel", "arbitrary")),
        name="attn",
    )(prep["score_bound"], q, k, v, prep["lam"], prep["sub_norm"])


class _GlaScan:
    def __init__(self, reverse, factored, nchunk, q_ref, k_ref, v_ref, g_ref, o_ref):
        self.reverse, self.factored, self.nchunk = reverse, factored, nchunk
        self.q_ref, self.k_ref, self.v_ref, self.g_ref, self.o_ref = q_ref, k_ref, v_ref, g_ref, o_ref
        c_rows = GLA_CHUNK
        ti = lax.broadcasted_iota(jnp.int32, (2 * c_rows, c_rows), 0)
        si = lax.broadcasted_iota(jnp.int32, (2 * c_rows, c_rows), 1)
        t_loc = ti % c_rows
        incl = self.before(si, t_loc)
        blk = jnp.logical_and(self.before(si // GLA_SUB, t_loc // GLA_SUB), (si // GLA_SUB) != (t_loc // GLA_SUB))
        self.cum_mat = jnp.where(ti < c_rows, jnp.where(incl, 1.0, 0.0), jnp.where(blk, 1.0, 0.0)).astype(BF16)
        hr = lax.broadcasted_iota(jnp.int32, (GLA_WIDTH, GLA_WIDTH), 0) // GLA_HEAD_DIM
        hc = lax.broadcasted_iota(jnp.int32, (GLA_WIDTH, GLA_WIDTH), 1) // GLA_HEAD_DIM
        self.head_mask = hr == hc
        tt = lax.broadcasted_iota(jnp.int32, (c_rows, GLA_WIDTH), 0)
        ss = lax.broadcasted_iota(jnp.int32, (c_rows, GLA_WIDTH), 1) % c_rows
        self.causal = self.before(ss, tt)
        self.row_blk = lax.broadcasted_iota(jnp.int32, (c_rows, 1), 0) // GLA_SUB
        if not factored:
            pair = (c_rows, c_rows, GLA_WIDTH)
            self.pair_mask = self.before(lax.broadcasted_iota(jnp.int32, pair, 1),
                                         lax.broadcasted_iota(jnp.int32, pair, 0))
            self.head_ones = jnp.where(self.head_mask, 1.0, 0.0).astype(BF16)

    def before(self, a, b_):
        return (a >= b_) if self.reverse else (a <= b_)

    def rows_of(self, ci):
        c = (self.nchunk - 1 - ci) if self.reverse else ci
        if isinstance(c, int):
            return slice(c * GLA_CHUNK, (c + 1) * GLA_CHUNK)
        return pl.ds(pl.multiple_of(c * GLA_CHUNK, GLA_CHUNK), GLA_CHUNK)

    def stage_cumsum(self, ci):
        g_hi, g_lo = _split2(self.g_ref[self.rows_of(ci), :])
        return _dot(self.cum_mat, g_hi) + _dot(self.cum_mat, g_lo)

    def stage_intra(self, ci, cums):
        rows = self.rows_of(ci)
        q = self.q_ref[rows, :].astype(F32)
        k = self.k_ref[rows, :].astype(F32)
        b_incl = cums[:GLA_CHUNK]
        b_ref = cums[GLA_CHUNK:]
        b_last = b_incl[0:1] if self.reverse else b_incl[GLA_CHUNK - 1:GLA_CHUNK]
        qe = (q * jnp.exp(b_incl)).astype(BF16)
        kd = (k * jnp.exp(b_last - b_incl)).astype(BF16)
        if not self.factored:
            decay = jnp.exp(jnp.minimum(b_incl[:, None, :] - b_incl[None, :, :], 0.0))
            x = jnp.where(self.pair_mask, q[:, None, :] * k[None, :, :] * decay, 0.0)
            y = _dot(x.reshape(GLA_CHUNK * GLA_CHUNK, GLA_WIDTH).astype(BF16), self.head_ones)
            v = self.v_ref[rows, :].astype(F32)
            o_intra = jnp.sum(y.reshape(GLA_CHUNK, GLA_CHUNK, GLA_WIDTH) * v[None, :, :], axis=1)
            return o_intra, qe, kd, jnp.exp(b_last)
        qs = q * jnp.exp(b_incl - b_ref)
        lhs_parts, rhs_parts = [], []
        for i in range(GLA_NSUB):
            r_i = b_ref[GLA_SUB * i:GLA_SUB * i + 1]
            k_i = jnp.where(self.before(self.row_blk, i), k * jnp.exp(r_i - b_incl), 0.0)
            k_i4 = jnp.concatenate([k_i] * GLA_HEADS, axis=0)
            rhs_parts.append(jnp.where(self.head_mask, k_i4, 0.0).astype(BF16))
            lhs_parts.append(jnp.where(self.row_blk == i, qs, 0.0).astype(BF16))
        lhs = jnp.concatenate(lhs_parts, axis=1)
        rhs = jnp.concatenate(rhs_parts, axis=1)
        a = jnp.where(self.causal, _dot_nt(lhs, rhs), 0.0).astype(BF16)
        return a, qe, kd, jnp.exp(b_last)

    def stage_state(self, ci, st, intra, qe, kd, decay):
        rows = self.rows_of(ci)
        v = self.v_ref[rows, :]
        if self.factored:
            v4 = jnp.where(self.head_mask, jnp.concatenate([v] * GLA_HEADS, axis=0), jnp.zeros((), BF16))
            intra = _dot(intra, v4)
        self.o_ref[rows, :] = intra + _dot_nt(qe, st.astype(BF16))
        return st * decay + jnp.where(self.head_mask, _dot_tn(v, kd), 0.0)


def _gla_kernel(nchunk, gmax_ref, qf_ref, kf_ref, vf_ref, gf_ref, qb_ref, kb_ref, vb_ref, gb_ref, of_ref, ob_ref,
                stf_ref, stb_ref):
    @pl.when(pl.program_id(1) == 0)
    def _():
        stf_ref[...] = jnp.zeros_like(stf_ref)
        stb_ref[...] = jnp.zeros_like(stb_ref)

    def run(factored):
        scans = (_GlaScan(False, factored, nchunk, qf_ref, kf_ref, vf_ref, gf_ref, of_ref),
                 _GlaScan(True, factored, nchunk, qb_ref, kb_ref, vb_ref, gb_ref, ob_ref))
        if not factored:
            def chunk(ci, carry):
                for scan, st_ref in zip(scans, (stf_ref, stb_ref)):
                    st_ref[...] = scan.stage_state(ci, st_ref[...], *scan.stage_intra(ci, scan.stage_cumsum(ci)))
                return carry

            lax.fori_loop(0, nchunk, chunk, 0)
            return
        states = [stf_ref[...], stb_ref[...]]
        cums, intra = [{}, {}], [{}, {}]
        for step in range(nchunk + 2):
            for d, scan in enumerate(scans):
                if step < nchunk:
                    cums[d][step] = scan.stage_cumsum(step)
            for d, scan in enumerate(scans):
                if 0 <= step - 1 < nchunk:
                    intra[d][step - 1] = scan.stage_intra(step - 1, cums[d].pop(step - 1))
            for d, scan in enumerate(scans):
                if 0 <= step - 2 < nchunk:
                    states[d] = scan.stage_state(step - 2, states[d], *intra[d].pop(step - 2))
        stf_ref[...] = states[0]
        stb_ref[...] = states[1]

    factored_ok = gmax_ref[pl.program_id(0)] * GLA_SUB <= GLA_FACTORED_MAX_EXP
    pl.when(factored_ok)(lambda: run(True))
    pl.when(jnp.logical_not(factored_ok))(lambda: run(False))


def _gla(gq, gk, gv, g_fwd, g_bwd, g_abs_max, plan):
    b, lp, _ = gq.shape
    tt = plan["gla_tile"]
    nt = lp // tt
    fwd = pl.BlockSpec((None, tt, GLA_WIDTH), lambda bb, i: (bb, i, 0))
    bwd = pl.BlockSpec((None, tt, GLA_WIDTH), lambda bb, i: (bb, nt - 1 - i, 0))
    out = jax.ShapeDtypeStruct((b, lp, GLA_WIDTH), F32)
    return pl.pallas_call(
        functools.partial(_gla_kernel, tt // GLA_CHUNK),
        grid=(b, nt),
        in_specs=[pl.BlockSpec(memory_space=pltpu.SMEM), fwd, fwd, fwd, fwd, bwd, bwd, bwd, bwd],
        out_specs=[fwd, bwd],
        out_shape=[out, out],
        scratch_shapes=[pltpu.VMEM((GLA_WIDTH, GLA_WIDTH), F32), pltpu.VMEM((GLA_WIDTH, GLA_WIDTH), F32)],
        compiler_params=_params(("parallel", "arbitrary")),
        name="gla",
    )(g_abs_max, gq, gk, gv, g_fwd, gq, gk, gv, g_bwd)


def _post_kernel(seq_len, tm, nt, h_ref, oa_ref, pu_ref, pup_ref, pun_ref, of_ref, ob_ref, gg_ref,
                 pw_ref, ps_ref, gn_ref, bd_ref, wo_ref, o_ref):
    i = pl.program_id(1)
    cur = pu_ref[...]
    prev = jnp.where(i > 0, pup_ref[...], 0.0)
    nxt = jnp.where(i < nt - 1, pun_ref[...], 0.0)
    ext = jnp.concatenate([prev, cur, nxt], axis=0)
    n = tm + 2 * POOL_HALO
    p2 = ext + pltpu.roll(ext, 1, 0)
    p4 = p2 + pltpu.roll(p2, 2, 0)
    p8 = p4 + pltpu.roll(p4, 4, 0)
    p16 = p8 + pltpu.roll(p8, 8, 0)
    sums = (p2, pltpu.roll(p4, n - 1, 0), pltpu.roll(p8, n - 3, 0), pltpu.roll(p16, n - 7, 0))

    pos = i * tm + lax.broadcasted_iota(jnp.int32, (tm, 1), 0)
    grp = lax.broadcasted_iota(jnp.int32, (tm, POOL_WIDTH), 1) // POOL_GROUP_DIM
    wsum = jnp.zeros((tm, POOL_WIDTH), F32)
    cnt = jnp.ones((tm, POOL_WIDTH), F32)
    for gi, win in enumerate(POOL_WINDOWS):
        lo = win // 2
        hi = win - lo - 1
        c = jnp.minimum(pos + hi + 1, seq_len) - jnp.maximum(pos - lo, 0)
        c = jnp.maximum(c, 1).astype(F32)
        sel = grp == gi
        wsum = jnp.where(sel, sums[gi][POOL_HALO:POOL_HALO + tm], wsum)
        cnt = jnp.where(sel, c, cnt)
    pooled = (wsum / cnt - cur).astype(BF16)
    y_pool = (_dot(pooled, pw_ref[...]) * ps_ref[...]).astype(BF16)

    og = of_ref[...] + ob_ref[...]
    og = og * lax.rsqrt(_group_mean_sq(og, bd_ref[...]) + RMS_EPS) * gn_ref[...]
    gate = gg_ref[...]
    y_gla = (og * (gate * jax.nn.sigmoid(gate))).astype(BF16)

    mixed = (_dot(oa_ref[...], wo_ref[0:ATT_WIDTH, :])
             + _dot(y_pool, wo_ref[ATT_WIDTH:ATT_WIDTH + POOL_WIDTH, :])
             + _dot(y_gla, wo_ref[ATT_WIDTH + POOL_WIDTH:MIX_WIDTH, :]))
    o_ref[...] = h_ref[...] + mixed


def _post(h, o_att, pu, o_f, o_b, gg, layer, seq_len, plan, prep):
    b, lp, _ = h.shape
    tm = plan["row_tile"]
    nt = lp // tm
    hb = tm // POOL_HALO
    row = lambda w: pl.BlockSpec((None, tm, w), lambda bb, i: (bb, i, 0))
    par = lambda *shape: pl.BlockSpec((None,) + shape, lambda bb, i: (layer,) + (0,) * len(shape))
    return pl.pallas_call(
        functools.partial(_post_kernel, seq_len, tm, nt),
        grid=(b, nt),
        in_specs=[
            row(D_MODEL),
            row(ATT_WIDTH),
            row(POOL_WIDTH),
            pl.BlockSpec((None, POOL_HALO, POOL_WIDTH), lambda bb, i: (bb, jnp.maximum(i * hb - 1, 0), 0)),
            pl.BlockSpec((None, POOL_HALO, POOL_WIDTH),
                         lambda bb, i: (bb, jnp.minimum((i + 1) * hb, lp // POOL_HALO - 1), 0)),
            row(GLA_WIDTH),
            row(GLA_WIDTH),
            row(GLA_WIDTH),
            par(POOL_WIDTH, POOL_WIDTH),
            par(1, POOL_WIDTH),
            par(1, GLA_WIDTH),
            pl.BlockSpec((V7X_MXU_DIM, V7X_MXU_DIM), lambda bb, i: (0, 0)),
            par(MIX_WIDTH, D_MODEL),
        ],
        out_specs=row(D_MODEL),
        out_shape=jax.ShapeDtypeStruct((b, lp, D_MODEL), F32),
        compiler_params=_params(("parallel", "parallel")),
        name="post",
    )(h, o_att, pu, pu, pu, o_f, o_b, gg, prep["pool_w"], prep["pool_scale"], prep["gla_norm"], prep["bd"],
      prep["w_out"])


def _rope_tables(lp):
    pos = jnp.arange(lp, dtype=F32)
    inv_freq = ROPE_THETA ** (-jnp.arange(0, ATT_HEAD_DIM, 2, dtype=F32) / ATT_HEAD_DIM)
    ang = pos[:, None] * inv_freq[None, :]
    cos, sin = jnp.cos(ang), jnp.sin(ang)
    return jnp.tile(cos, (1, 4)), jnp.tile(jnp.concatenate([-sin, sin], axis=1), (1, 2))


def _prepare(p, padded_lengths):
    eye = jnp.eye(V7X_MXU_DIM // ATT_HEAD_DIM, dtype=F32)
    bd = jnp.kron(eye, jnp.full((ATT_HEAD_DIM, ATT_HEAD_DIM), 1.0 / ATT_HEAD_DIM, F32)).astype(BF16)
    w_in = jnp.pad(p["w_in"], ((0, 0), (0, 0), (0, IN_COLS_PAD - IN_COLS))).astype(BF16)
    gu = jnp.zeros((DEPTH, V7X_LANES, 2 * GLA_WIDTH), F32)
    gu = gu.at[:, :GLA_GATE_RANK, :GLA_WIDTH].set(p["gla_gate_up"][:, 0])
    gu = gu.at[:, GLA_GATE_RANK:2 * GLA_GATE_RANK, GLA_WIDTH:].set(p["gla_gate_up"][:, 1])
    pool_w = jnp.zeros((DEPTH, POOL_WIDTH, POOL_WIDTH), F32)
    for gi in range(len(POOL_WINDOWS)):
        sl = slice(gi * POOL_GROUP_DIM, (gi + 1) * POOL_GROUP_DIM)
        pool_w = pool_w.at[:, sl, sl].set(p["pool_w"][:, gi])
    tables = {lp: _rope_tables(lp) for lp in padded_lengths}
    return dict(
        bd=bd,
        w_in=w_in,
        w_out=p["w_out"].astype(BF16),
        mix_norm=p["mix_norm"][:, None, :],
        q_gain=jnp.tile(p["attn_q_norm"], (1, 2))[:, None, :],
        k_gain=jnp.tile(p["attn_k_norm"], (1, 2))[:, None, :],
        lam=jnp.stack([p["lambda_q1"], p["lambda_k1"], p["lambda_q2"], p["lambda_k2"]], axis=1),
        sub_norm=p["attn_sub_norm"][:, None, :],
        score_bound=(1.02 * ATT_HEAD_DIM * ATT_HEAD_DIM ** -0.5 * LOG2E
                     * jnp.max(jnp.abs(p["attn_q_norm"]), axis=1) * jnp.max(jnp.abs(p["attn_k_norm"]), axis=1)),
        gate_up=gu,
        gate_bias=p["gla_gate_bias"].reshape(DEPTH, 1, 2 * GLA_WIDTH),
        pool_w=pool_w.astype(BF16),
        pool_scale=p["pool_scale"][:, None, :],
        gla_norm=jnp.tile(p["gla_out_norm"], (1, GLA_HEADS))[:, None, :],
        ffn1_norm=p["ffn1_norm"][:, None, :],
        ffn2_norm=p["ffn2_norm"][:, None, :],
        ffn1_w_in=p["ffn1_w_in"].astype(BF16),
        ffn1_w_out=p["ffn1_w_out"].astype(BF16),
        ffn2_w_in=p["ffn2_w_in"].astype(BF16),
        ffn2_w_out=p["ffn2_w_out"].astype(BF16),
        cos={lp: t[0] for lp, t in tables.items()},
        sin={lp: t[1] for lp, t in tables.items()},
    )


def _trunk(x, meta_tokens, prep):
    b, s, _ = x.shape
    seq_len = s + N_META
    plan = _plan(seq_len)
    lp = plan["lp"]
    for layer in range(DEPTH):
        lam_init = 0.8 - 0.6 * math.exp(-0.3 * layer)
        if layer == 0:
            h = _ffn_embed(x, meta_tokens.astype(x.dtype), lp, layer, prep["ffn1_norm"], prep["ffn1_w_in"],
                           prep["ffn1_w_out"])
        else:
            h = _ffn(h.reshape(b * lp, D_MODEL), layer, prep["ffn1_norm"], prep["ffn1_w_in"],
                     prep["ffn1_w_out"]).reshape(b, lp, D_MODEL)
        q, k, v, pu, gq, gk, gv, gg, gf, gb, gmax = _proj(h, layer, seq_len, plan, prep)
        o_att = _attn(q, k, v, layer, seq_len, plan, prep, lam_init)
        o_f, o_b = _gla(gq, gk, gv, gf, gb, jnp.max(gmax, axis=(1, 2, 3)), plan)
        h = _post(h, o_att, pu, o_f, o_b, gg, layer, seq_len, plan, prep)
        if layer == DEPTH - 1:
            return _ffn_unpad(h, s, layer, prep["ffn2_norm"], prep["ffn2_w_in"], prep["ffn2_w_out"])
        h = _ffn(h.reshape(b * lp, D_MODEL), layer, prep["ffn2_norm"], prep["ffn2_w_in"],
                 prep["ffn2_w_out"]).reshape(b, lp, D_MODEL)


def kernel(x_prompt, x_sample, meta_tokens, ffn1_norm, ffn1_w_in, ffn1_w_out, mix_norm, w_in, w_out, attn_q_norm,
           attn_k_norm, lambda_q1, lambda_k1, lambda_q2, lambda_k2, attn_sub_norm, pool_w, pool_scale, gla_gate_up,
           gla_gate_bias, gla_out_norm, ffn2_norm, ffn2_w_in, ffn2_w_out):
    p = dict(ffn1_norm=ffn1_norm, ffn1_w_in=ffn1_w_in, ffn1_w_out=ffn1_w_out, mix_norm=mix_norm, w_in=w_in,
             w_out=w_out, attn_q_norm=attn_q_norm, attn_k_norm=attn_k_norm, lambda_q1=lambda_q1,
             lambda_k1=lambda_k1, lambda_q2=lambda_q2, lambda_k2=lambda_k2, attn_sub_norm=attn_sub_norm,
             pool_w=pool_w, pool_scale=pool_scale, gla_gate_up=gla_gate_up, gla_gate_bias=gla_gate_bias,
             gla_out_norm=gla_out_norm, ffn2_norm=ffn2_norm, ffn2_w_in=ffn2_w_in, ffn2_w_out=ffn2_w_out)
    lengths = {_plan(x.shape[1] + N_META)["lp"] for x in (x_prompt, x_sample)}
    prep = _prepare(p, lengths)
    return (_trunk(x_prompt, meta_tokens, prep), _trunk(x_sample, meta_tokens, prep))
```
